```python
import jax
import jax.numpy as jnp
from jax import lax
import numpy as np

D_MODEL = 1024
BATCH = 8
SEQ = 2048
DEPTH = 4
DEC_BATCH = 128
DEC_SEQ = 8
PAST_LEN = 8192
PAGE_SIZE = 128

N_HEADS = 16
HEAD_DIM = 64
Q_WIDTH = N_HEADS * HEAD_DIM
D_FF = 2816
N_MIXERS = 3
RMS_EPS = 1e-6
ATTN_SCALE = HEAD_DIM ** -0.5
KV_A = 2
CMP_LEN = 32
CMP_STRIDE = 16
CMP_HID = 256
SEL_BLOCK = 64
SEL_TOPK = 16
WIN_A = 512
KV_B = 2
WIN_B = 128
KV_C = 4
MOBA_BLOCK = 256
MOBA_TOPK = 3
Q_BLOCK = 128
SEL_Q_BLOCK = 32
MOBA_Q_BLOCK = 16
N_NSA = (DEPTH + 2) // 3
N_SWA = (DEPTH + 1) // 3
N_MOBA = DEPTH // 3

kernel_name = 'hybrid_nsa_swa_moba_decode_step'


def rmsnorm(x, g):
    xf = x.astype(jnp.float32)
    y = xf * lax.rsqrt(jnp.mean(xf * xf, axis=-1, keepdims=True) + RMS_EPS)
    return (y * g.astype(jnp.float32)).astype(x.dtype)


def swiglu(x, wg, wu, wd):
    return (jax.nn.silu(x @ wg) * (x @ wu)) @ wd


def macaron_ffn(x, g_pre, g_post, wg, wu, wd):
    return x + 0.5 * rmsnorm(swiglu(rmsnorm(x, g_pre), wg, wu, wd), g_post)


def alibi_slopes(n_heads):
    return jnp.exp2(-8.0 * jnp.arange(1, n_heads + 1, dtype=jnp.float32) / n_heads)


def masked_softmax(s, mask, sink=None):
    s = jnp.where(mask, s, -jnp.inf)
    m = jnp.max(s, axis=-1, keepdims=True)
    if sink is not None:
        m = jnp.maximum(m, sink)
    m = jnp.where(jnp.isfinite(m), m, 0.0)
    e = jnp.exp(s - m)
    den = jnp.sum(e, axis=-1, keepdims=True)
    if sink is not None:
        den = den + jnp.exp(sink - m)
    return e / jnp.maximum(den, jnp.finfo(jnp.float32).tiny)


def gather_tokens(new, pool, ptab, pos, grp):
    n, s = new.shape[0], new.shape[1]
    nidx = jnp.arange(n).reshape((n,) + (1,) * (pos.ndim - 1))
    if pool is None:
        return new[nidx, jnp.clip(pos, 0, s - 1), grp]
    psz = pool.shape[1]
    past = ptab.shape[1] * psz
    pp = jnp.clip(pos, 0, past - 1)
    old = pool[ptab[nidx, pp // psz], pp % psz, grp]
    cur = new[nidx, jnp.clip(pos - past, 0, s - 1), grp]
    return jnp.where((pos < past)[..., None], old, cur)


def gather_past(pool, ptab, new):
    n = ptab.shape[0]
    past = pool[ptab].reshape((n, -1) + pool.shape[2:])
    return jnp.concatenate([past, new], axis=1)


def map_qblocks(fn, qblk, qpos, *xs):
    t = qpos.shape[0]
    nblk = t // qblk
    split = lambda a: jnp.swapaxes(a.reshape((a.shape[0], nblk, qblk) + a.shape[2:]), 0, 1)
    out = lax.map(lambda args: fn(*args), (qpos.reshape(nblk, qblk),) + tuple(split(a) for a in xs))
    return jnp.swapaxes(out, 0, 1).reshape((out.shape[1], t) + out.shape[3:])


def local_attn(q, qpos, k, v, kpos, slopes, window, sinks=None):
    n, nq, h, hd = q.shape
    g = k.shape[2]
    qg = q.reshape(n, nq, g, h // g, hd)
    s = jnp.einsum('nqghd,nkgd->nghqk', qg, k, preferred_element_type=jnp.float32) * ATTN_SCALE
    dist = qpos[:, None] - kpos[None, :]
    mask = (dist >= 0) & (dist < window) & (kpos >= 0)[None, :]
    s = s - slopes.reshape(g, h // g)[:, :, None, None] * dist.astype(jnp.float32)
    sink = None if sinks is None else sinks.astype(jnp.float32).reshape(g, h // g)[:, :, None, None]
    p = masked_softmax(s, mask, sink)
    o = jnp.einsum('nghqk,nkgd->nqghd', p.astype(v.dtype), v, preferred_element_type=jnp.float32)
    return o.reshape(n, nq, h, hd).astype(q.dtype)


def banded_prompt(q, k, v, slopes, window, sinks=None):
    t = q.shape[1]
    pad = -(-(window - 1) // Q_BLOCK) * Q_BLOCK
    kp = jnp.pad(k, ((0, 0), (pad, 0), (0, 0), (0, 0)))
    vp = jnp.pad(v, ((0, 0), (pad, 0), (0, 0), (0, 0)))
    span = pad + Q_BLOCK

    def blk(qpos, qb):
        start = qpos[0]
        kb = lax.dynamic_slice_in_dim(kp, start, span, axis=1)
        vb = lax.dynamic_slice_in_dim(vp, start, span, axis=1)
        kpos = start - pad + jnp.arange(span)
        return local_attn(qb, qpos, kb, vb, kpos, slopes, window, sinks)

    return map_qblocks(blk, Q_BLOCK, jnp.arange(t), q)


def nsa_compress(k, pos_emb, w1, w2):
    n, length, g, hd = k.shape
    r = CMP_LEN // CMP_STRIDE
    nch = length // CMP_STRIDE
    nc = nch - r + 1
    ch = k[:, :nch * CMP_STRIDE].reshape(n, nch, CMP_STRIDE, g, hd)
    w1r = w1.reshape(r, CMP_STRIDE, hd, CMP_HID)
    hid = jnp.einsum('ld,ldh->h', pos_emb, w1.reshape(CMP_LEN, hd, CMP_HID))
    for j in range(r):
        hid = hid + jnp.einsum('ncsgd,sdh->ncgh', ch[:, j:j + nc], w1r[j])
    return jax.nn.gelu(hid) @ w2


def nsa_cmp_attn(q, qpos, kc, vc, slopes):
    n, nq, h, hd = q.shape
    g, nc = kc.shape[2], kc.shape[1]
    cend = jnp.arange(nc) * CMP_STRIDE + CMP_LEN - 1
    qg = q.reshape(n, nq, g, h // g, hd)
    s = jnp.einsum('nqghd,ncgd->nghqc', qg, kc, preferred_element_type=jnp.float32) * ATTN_SCALE
    dist = qpos[:, None] - cend[None, :]
    s = s - slopes.reshape(g, h // g)[:, :, None, None] * dist.astype(jnp.float32)
    p = masked_softmax(s, dist >= 0)
    o = jnp.einsum('nghqc,ncgd->nqghd', p.astype(vc.dtype), vc, preferred_element_type=jnp.float32)
    return o.reshape(n, nq, h, hd).astype(q.dtype), jnp.sum(p, axis=2)


def nsa_select(imp, qpos, length):
    nc = imp.shape[-1]
    nsb = -(-length // SEL_BLOCK)
    cstart = jnp.arange(nc) * CMP_STRIDE
    sstart = jnp.arange(nsb) * SEL_BLOCK
    overlap = ((cstart[:, None] < sstart[None, :] + SEL_BLOCK)
               & (cstart[:, None] + CMP_LEN > sstart[None, :])).astype(jnp.float32)
    score = jnp.einsum('ngqc,cj->nqgj', imp, overlap)
    qb = (qpos // SEL_BLOCK)[:, None]
    j = jnp.arange(nsb)[None, :]
    forced = ((j == 0) | (j == qb) | (j == qb - 1))[None, :, None, :]
    causal = (j <= qb)[None, :, None, :]
    score = jnp.where(forced, jnp.inf, jnp.where(causal, score, -jnp.inf))
    val, idx = lax.top_k(score, min(SEL_TOPK, nsb))
    return idx, val > -jnp.inf


def nsa_sel_attn(q, qpos, bidx, bval, new_k, new_v, pool_k, pool_v, ptab, slopes):
    n, nq, h, hd = q.shape
    g, kk = new_k.shape[2], bidx.shape[-1]
    kpos = (bidx[..., None] * SEL_BLOCK + jnp.arange(SEL_BLOCK)).reshape(n, nq, g, kk * SEL_BLOCK)
    grp = jnp.arange(g)[None, None, :, None]
    ks = gather_tokens(new_k, pool_k, ptab, kpos, grp)
    vs = gather_tokens(new_v, pool_v, ptab, kpos, grp)
    qg = q.reshape(n, nq, g, h // g, hd)
    s = jnp.einsum('nqghd,nqgkd->nqghk', qg, ks, preferred_element_type=jnp.float32) * ATTN_SCALE
    dist = qpos[None, :, None, None] - kpos
    mask = (dist >= 0) & jnp.repeat(bval, SEL_BLOCK, axis=-1)
    s = s - slopes.reshape(g, h // g)[:, :, None] * dist[:, :, :, None, :].astype(jnp.float32)
    p = masked_softmax(s, mask[:, :, :, None, :])
    o = jnp.einsum('nqghk,nqgkd->nqghd', p.astype(vs.dtype), vs, preferred_element_type=jnp.float32)
    return o.reshape(n, nq, h, hd).astype(q.dtype)


def nsa_project(h, w_in):
    n, t, _ = h.shape
    kvw = KV_A * HEAD_DIM
    proj = h @ w_in
    q = proj[..., :Q_WIDTH].reshape(n, t, N_HEADS, HEAD_DIM)
    kv = proj[..., Q_WIDTH:Q_WIDTH + 6 * kvw].reshape(n, t, 6, KV_A, HEAD_DIM)
    gates = jax.nn.sigmoid(proj[..., Q_WIDTH + 6 * kvw:].astype(jnp.float32)).reshape(n, t, 3, N_HEADS, 1)
    return q, [kv[:, :, i] for i in range(6)], gates


def nsa_merge(o_c, o_s, o_w, gates, w_out):
    n, t = o_c.shape[:2]
    o = gates[:, :, 0] * o_c + gates[:, :, 1] * o_s + gates[:, :, 2] * o_w
    return o.astype(w_out.dtype).reshape(n, t, Q_WIDTH) @ w_out


def nsa_prompt(h, w_in, cmp_w, w_out, slopes):
    t = h.shape[1]
    q, (kc, vc, ks, vs, kw, vw), gates = nsa_project(h, w_in)
    qpos = jnp.arange(t)
    o_c, imp = nsa_cmp_attn(q, qpos, nsa_compress(kc, *cmp_w[:3]), nsa_compress(vc, *cmp_w[3:]), slopes)
    bidx, bval = nsa_select(imp, qpos, t)
    o_s = map_qblocks(lambda qp, qb, bi, bv: nsa_sel_attn(qb, qp, bi, bv, ks, vs, None, None, None, slopes),
                      SEL_Q_BLOCK, qpos, q, bidx, bval)
    o_w = banded_prompt(q, kw, vw, slopes, WIN_A)
    wb = min(WIN_A, t)
    return nsa_merge(o_c, o_s, o_w, gates, w_out), (kc, vc, ks, vs, kw[:, t - wb:], vw[:, t - wb:])


def nsa_sample(h, pool_ck, pool_cv, pool_sk, pool_sv, buf_k, buf_v, ptab, w_in, cmp_w, w_out, slopes):
    s = h.shape[1]
    past = ptab.shape[1] * pool_ck.shape[1]
    q, (kc, vc, ks, vs, kw, vw), gates = nsa_project(h, w_in)
    qpos = past + jnp.arange(s)
    kcmp = nsa_compress(gather_past(pool_ck, ptab, kc), *cmp_w[:3])
    vcmp = nsa_compress(gather_past(pool_cv, ptab, vc), *cmp_w[3:])
    o_c, imp = nsa_cmp_attn(q, qpos, kcmp, vcmp, slopes)
    bidx, bval = nsa_select(imp, qpos, past + s)

    def one(qq, pt, nk, nv, bi, bv):
        return nsa_sel_attn(qq[None], qpos, bi[None], bv[None], nk[None], nv[None],
                            pool_sk, pool_sv, pt[None], slopes)[0]

    o_s = lax.map(lambda a: one(*a), (q, ptab, ks, vs, bidx, bval))
    wb = buf_k.shape[1]
    kw_all = jnp.concatenate([buf_k, kw], axis=1)
    vw_all = jnp.concatenate([buf_v, vw], axis=1)
    o_w = local_attn(q, qpos, kw_all, vw_all, past - wb + jnp.arange(wb + s), slopes, WIN_A)
    return nsa_merge(o_c, o_s, o_w, gates, w_out), (kc, vc, ks, vs, kw_all[:, s:], vw_all[:, s:])


def gqa_project(h, w_in):
    n, t, _ = h.shape
    proj = h @ w_in
    q = proj[..., :Q_WIDTH].reshape(n, t, N_HEADS, HEAD_DIM)
    kv = proj[..., Q_WIDTH:].reshape(n, t, 2, -1, HEAD_DIM)
    return q, kv[:, :, 0], kv[:, :, 1]


def swa_prompt(h, w_in, sinks, w_out, slopes):
    n, t, _ = h.shape
    q, k, v = gqa_project(h, w_in)
    o = banded_prompt(q, k, v, slopes, WIN_B, sinks)
    wb = min(WIN_B, t)
    return o.reshape(n, t, Q_WIDTH) @ w_out, (k[:, t - wb:], v[:, t - wb:])


def swa_sample(h, buf_k, buf_v, past, w_in, sinks, w_out, slopes):
    n, s, _ = h.shape
    q, k, v = gqa_project(h, w_in)
    wb = buf_k.shape[1]
    k_all = jnp.concatenate([buf_k, k], axis=1)
    v_all = jnp.concatenate([buf_v, v], axis=1)
    o = local_attn(q, past + jnp.arange(s), k_all, v_all, past - wb + jnp.arange(wb + s), slopes, WIN_B, sinks)
    return o.reshape(n, s, Q_WIDTH) @ w_out, (k_all[:, s:], v_all[:, s:])


def block_means(k):
    n, length, g, hd = k.shape
    nb = -(-length // MOBA_BLOCK)
    kp = jnp.pad(k, ((0, 0), (0, nb * MOBA_BLOCK - length), (0, 0), (0, 0)))
    return jnp.mean(kp.reshape(n, nb, MOBA_BLOCK, g, hd).astype(jnp.float32), axis=2)


def moba_attn(q, qpos, kmean, new_k, new_v, pool_k, pool_v, ptab, slopes):
    n, nq, h, hd = q.shape
    g = new_k.shape[2]
    hg = h // g
    nb = kmean.shape[1]
    qg = q.reshape(n, nq, g, hg, hd)
    gate = jnp.einsum('nqghd,nbgd->nqghb', qg.astype(jnp.float32), kmean)
    qb = qpos // MOBA_BLOCK
    fully_past = (jnp.arange(nb)[None, :] < qb[:, None])[None, :, None, None, :]
    val, idx = lax.top_k(jnp.where(fully_past, gate, -jnp.inf), min(MOBA_TOPK, nb))
    own = jnp.broadcast_to(qb[None, :, None, None, None], (n, nq, g, hg, 1))
    bidx = jnp.concatenate([idx, own], axis=-1)
    bval = jnp.concatenate([val > -jnp.inf, jnp.ones(own.shape, dtype=bool)], axis=-1)
    kpos = (bidx[..., None] * MOBA_BLOCK + jnp.arange(MOBA_BLOCK)).reshape(n, nq, g, hg, -1)
    grp = jnp.arange(g)[None, None, :, None, None]
    ks = gather_tokens(new_k, pool_k, ptab, kpos, grp)
    vs = gather_tokens(new_v, pool_v, ptab, kpos, grp)
    s = jnp.einsum('nqghd,nqghkd->nqghk', qg, ks, preferred_element_type=jnp.float32) * ATTN_SCALE
    dist = qpos[None, :, None, None, None] - kpos
    mask = (dist >= 0) & jnp.repeat(bval, MOBA_BLOCK, axis=-1)
    s = s - slopes.reshape(g, hg)[:, :, None] * dist.astype(jnp.float32)
    p = masked_softmax(s, mask)
    o = jnp.einsum('nqghk,nqghkd->nqghd', p.astype(vs.dtype), vs, preferred_element_type=jnp.float32)
    return o.reshape(n, nq, h, hd).astype(q.dtype)


def moba_prompt(h, w_in, w_out, slopes):
    n, t, _ = h.shape
    q, k, v = gqa_project(h, w_in)
    kmean = block_means(k)
    o = map_qblocks(lambda qp, qq: moba_attn(qq, qp, kmean, k, v, None, None, None, slopes),
                    MOBA_Q_BLOCK, jnp.arange(t), q)
    return o.reshape(n, t, Q_WIDTH) @ w_out, (k, v)


def moba_sample(h, pool_k, pool_v, ptab, w_in, w_out, slopes):
    n, s, _ = h.shape
    past = ptab.shape[1] * pool_k.shape[1]
    q, k, v = gqa_project(h, w_in)
    kmean = block_means(gather_past(pool_k, ptab, k))
    qpos = past + jnp.arange(s)

    def one(qq, km, nk, nv, pt):
        return moba_attn(qq[None], qpos, km[None], nk[None], nv[None], pool_k, pool_v, pt[None], slopes)[0]

    o = lax.map(lambda a: one(*a), (q, kmean, k, v, ptab))
    return o.reshape(n, s, Q_WIDTH) @ w_out, (k, v)


def setup_inputs(seed: int = 0):
    keys = list(jax.random.split(jax.random.key(seed), 40))

    def nrm(i, shape, scale):
        return jax.random.normal(keys[i], shape, jnp.float32) * scale

    d = D_MODEL
    n_pages = PAST_LEN // PAGE_SIZE
    n_used = DEC_BATCH * n_pages
    n_phys = n_used + max(1, n_used // 4)
    wb_a, wb_b = min(WIN_A, PAST_LEN), min(WIN_B, PAST_LEN)
    in_a = Q_WIDTH + 6 * KV_A * HEAD_DIM + 3 * N_HEADS
    in_b = Q_WIDTH + 2 * KV_B * HEAD_DIM
    in_c = Q_WIDTH + 2 * KV_C * HEAD_DIM
    page_table = jax.random.permutation(keys[0], n_phys)[:n_used].reshape(DEC_BATCH, n_pages).astype(jnp.int32)
    return {
        'x_prompt': nrm(1, (BATCH, SEQ, d), 1.0),
        'x_sample': nrm(2, (DEC_BATCH, DEC_SEQ, d), 1.0),
        'cache_nsa_cmp_k': nrm(3, (N_NSA, n_phys, PAGE_SIZE, KV_A, HEAD_DIM), 1.0),
        'cache_nsa_cmp_v': nrm(4, (N_NSA, n_phys, PAGE_SIZE, KV_A, HEAD_DIM), 1.0),
        'cache_nsa_sel_k': nrm(5, (N_NSA, n_phys, PAGE_SIZE, KV_A, HEAD_DIM), 1.0),
        'cache_nsa_sel_v': nrm(6, (N_NSA, n_phys, PAGE_SIZE, KV_A, HEAD_DIM), 1.0),
        'cache_nsa_win_k': nrm(7, (N_NSA, DEC_BATCH, wb_a, KV_A, HEAD_DIM), 1.0),
        'cache_nsa_win_v': nrm(8, (N_NSA, DEC_BATCH, wb_a, KV_A, HEAD_DIM), 1.0),
        'cache_swa_k': nrm(9, (N_SWA, DEC_BATCH, wb_b, KV_B, HEAD_DIM), 1.0),
        'cache_swa_v': nrm(10, (N_SWA, DEC_BATCH, wb_b, KV_B, HEAD_DIM), 1.0),
        'cache_moba_k': nrm(11, (N_MOBA, n_phys, PAGE_SIZE, KV_C, HEAD_DIM), 1.0),
        'cache_moba_v': nrm(12, (N_MOBA, n_phys, PAGE_SIZE, KV_C, HEAD_DIM), 1.0),
        'page_table': page_table,
        'norm_pre': 1.0 + nrm(13, (DEPTH, 3, d), 0.1),
        'norm_post': 1.0 + nrm(14, (DEPTH, 3, d), 0.1),
        'ffn_w_gate': nrm(15, (DEPTH, 2, d, D_FF), d ** -0.5),
        'ffn_w_up': nrm(16, (DEPTH, 2, d, D_FF), d ** -0.5),
        'ffn_w_down': nrm(17, (DEPTH, 2, D_FF, d), D_FF ** -0.5),
        'nsa_w_in': nrm(18, (N_NSA, d, in_a), d ** -0.5),
        'nsa_cmp_pos_k': nrm(19, (N_NSA, CMP_LEN, HEAD_DIM), 0.1),
        'nsa_cmp_w1_k': nrm(20, (N_NSA, CMP_LEN * HEAD_DIM, CMP_HID), (CMP_LEN * HEAD_DIM) ** -0.5),
        'nsa_cmp_w2_k': nrm(21, (N_NSA, CMP_HID, HEAD_DIM), CMP_HID ** -0.5),
        'nsa_cmp_pos_v': nrm(22, (N_NSA, CMP_LEN, HEAD_DIM), 0.1),
        'nsa_cmp_w1_v': nrm(23, (N_NSA, CMP_LEN * HEAD_DIM, CMP_HID), (CMP_LEN * HEAD_DIM) ** -0.5),
        'nsa_cmp_w2_v': nrm(24, (N_NSA, CMP_HID, HEAD_DIM), CMP_HID ** -0.5),
        'nsa_w_out': nrm(25, (N_NSA, Q_WIDTH, d), Q_WIDTH ** -0.5),
        'swa_w_in': nrm(26, (N_SWA, d, in_b), d ** -0.5),
        'swa_sinks': nrm(27, (N_SWA, N_HEADS), 1.0),
        'swa_w_out': nrm(28, (N_SWA, Q_WIDTH, d), Q_WIDTH ** -0.5),
        'moba_w_in': nrm(29, (N_MOBA, d, in_c), d ** -0.5),
        'moba_w_out': nrm(30, (N_MOBA, Q_WIDTH, d), Q_WIDTH ** -0.5),
    }


def reference(x_prompt, x_sample, cache_nsa_cmp_k, cache_nsa_cmp_v, cache_nsa_sel_k, cache_nsa_sel_v,
              cache_nsa_win_k, cache_nsa_win_v, cache_swa_k, cache_swa_v, cache_moba_k, cache_moba_v,
              page_table, norm_pre, norm_post, ffn_w_gate, ffn_w_up, ffn_w_down,
              nsa_w_in, nsa_cmp_pos_k, nsa_cmp_w1_k, nsa_cmp_w2_k, nsa_cmp_pos_v, nsa_cmp_w1_v, nsa_cmp_w2_v,
              nsa_w_out, swa_w_in, swa_sinks, swa_w_out, moba_w_in, moba_w_out):
    slopes = alibi_slopes(N_HEADS)
    past = page_table.shape[1] * cache_moba_k.shape[2]
    xp, xs = x_prompt, x_sample
    nsa_p, nsa_s, swa_p, swa_s, moba_p, moba_s = [], [], [], [], [], []
    for i in range(DEPTH):
        kind, j = i % N_MIXERS, i // N_MIXERS
        xp = macaron_ffn(xp, norm_pre[i, 0], norm_post[i, 0], ffn_w_gate[i, 0], ffn_w_up[i, 0], ffn_w_down[i, 0])
        xs = macaron_ffn(xs, norm_pre[i, 0], norm_post[i, 0], ffn_w_gate[i, 0], ffn_w_up[i, 0], ffn_w_down[i, 0])
        hp = rmsnorm(xp, norm_pre[i, 1])
        hs = rmsnorm(xs, norm_pre[i, 1])
        if kind == 0:
            cmp_w = (nsa_cmp_pos_k[j], nsa_cmp_w1_k[j], nsa_cmp_w2_k[j],
                     nsa_cmp_pos_v[j], nsa_cmp_w1_v[j], nsa_cmp_w2_v[j])
            mp, stp = nsa_prompt(hp, nsa_w_in[j], cmp_w, nsa_w_out[j], slopes)
            ms, sts = nsa_sample(hs, cache_nsa_cmp_k[j], cache_nsa_cmp_v[j], cache_nsa_sel_k[j],
                                 cache_nsa_sel_v[j], cache_nsa_win_k[j], cache_nsa_win_v[j], page_table,
                                 nsa_w_in[j], cmp_w, nsa_w_out[j], slopes)
            nsa_p.append(stp)
            nsa_s.append(sts)
        elif kind == 1:
            mp, stp = swa_prompt(hp, swa_w_in[j], swa_sinks[j], swa_w_out[j], slopes)
            ms, sts = swa_sample(hs, cache_swa_k[j], cache_swa_v[j], past, swa_w_in[j], swa_sinks[j],
                                 swa_w_out[j], slopes)
            swa_p.append(stp)
            swa_s.append(sts)
        else:
            mp, stp = moba_prompt(hp, moba_w_in[j], moba_w_out[j], slopes)
            ms, sts = moba_sample(hs, cache_moba_k[j], cache_moba_v[j], page_table, moba_w_in[j],
                                  moba_w_out[j], slopes)
            moba_p.append(stp)
            moba_s.append(sts)
        xp = xp + rmsnorm(mp, norm_post[i, 1])
        xs = xs + rmsnorm(ms, norm_post[i, 1])
        xp = macaron_ffn(xp, norm_pre[i, 2], norm_post[i, 2], ffn_w_gate[i, 1], ffn_w_up[i, 1], ffn_w_down[i, 1])
        xs = macaron_ffn(xs, norm_pre[i, 2], norm_post[i, 2], ffn_w_gate[i, 1], ffn_w_up[i, 1], ffn_w_down[i, 1])
    nsa_cmp_k_p, nsa_cmp_v_p, nsa_sel_k_p, nsa_sel_v_p, nsa_win_k_p, nsa_win_v_p = [jnp.stack(a) for a in zip(*nsa_p)]
    nsa_cmp_k_s, nsa_cmp_v_s, nsa_sel_k_s, nsa_sel_v_s, nsa_win_k_s, nsa_win_v_s = [jnp.stack(a) for a in zip(*nsa_s)]
    swa_k_p, swa_v_p = [jnp.stack(a) for a in zip(*swa_p)]
    swa_k_s, swa_v_s = [jnp.stack(a) for a in zip(*swa_s)]
    moba_k_p, moba_v_p = [jnp.stack(a) for a in zip(*moba_p)]
    moba_k_s, moba_v_s = [jnp.stack(a) for a in zip(*moba_s)]
    y_prompt, y_sample = xp, xs
    return (y_prompt, y_sample,
            nsa_cmp_k_p, nsa_cmp_v_p, nsa_sel_k_p, nsa_sel_v_p, nsa_win_k_p, nsa_win_v_p,
            swa_k_p, swa_v_p, moba_k_p, moba_v_p,
            nsa_cmp_k_s, nsa_cmp_v_s, nsa_sel_k_s, nsa_sel_v_s, nsa_win_k_s, nsa_win_v_s,
            swa_k_s, swa_v_s, moba_k_s, moba_v_s)
```

```python
import functools

import jax
import jax.numpy as jnp
from jax import lax
from jax.experimental import pallas as pl
from jax.experimental.pallas import tpu as pltpu

F32 = jnp.float32
BF16 = jnp.bfloat16
I32 = jnp.int32

D_MODEL = 1024
N_HEADS = 16
HEAD_DIM = 64
Q_WIDTH = N_HEADS * HEAD_DIM
RMS_EPS = 1e-6
ATTN_SCALE = HEAD_DIM ** -0.5
KV_A, KV_B, KV_C = 2, 2, 4
CMP_LEN, CMP_STRIDE, CMP_HID = 32, 16, 256
SEL_BLOCK, SEL_TOPK = 64, 16
WIN_A, WIN_B = 512, 128
MOBA_BLOCK, MOBA_TOPK = 256, 3
PAGE = 128

LANES = 128
TINY = float(jnp.finfo(jnp.float32).tiny)
M_INIT = -1e30
VMEM_LIMIT = 56 * 1024 * 1024

TQ = 128
FFN_TM, FFN_TF = 1024, 256
PROJ_TM = 512


def _cparams(sem):
    return pltpu.CompilerParams(dimension_semantics=sem, vmem_limit_bytes=VMEM_LIMIT)


def _mm_nt(a, b):
    return lax.dot_general(a, b, (((1,), (1,)), ((), ())), preferred_element_type=F32)


def _rms(x, g):
    return x * lax.rsqrt(jnp.mean(x * x, axis=-1, keepdims=True) + RMS_EPS) * g


def _ffn_kernel(x_ref, gpre_ref, gpost_ref, wg_ref, wu_ref, wd_ref, o_ref, xn_ref, acc_ref):
    f = pl.program_id(1)

    @pl.when(f == 0)
    def _():
        xn_ref[...] = _rms(x_ref[...], gpre_ref[...]).astype(BF16)
        acc_ref[...] = jnp.zeros_like(acc_ref)

    xn = xn_ref[...]
    g = jnp.dot(xn, wg_ref[...], preferred_element_type=F32)
    u = jnp.dot(xn, wu_ref[...], preferred_element_type=F32)
    h = g * jax.nn.sigmoid(g) * u
    acc_ref[...] += jnp.dot(h.astype(BF16), wd_ref[...], preferred_element_type=F32)

    @pl.when(f == pl.num_programs(1) - 1)
    def _():
        o_ref[...] = x_ref[...] + 0.5 * _rms(acc_ref[...], gpost_ref[...])


def ffn(x, g_pre, g_post, wg, wu, wd):
    m, d = x.shape
    dff = wg.shape[1]
    tm = min(FFN_TM, m)
    return pl.pallas_call(
        _ffn_kernel,
        grid=(m // tm, dff // FFN_TF),
        in_specs=[
            pl.BlockSpec((tm, d), lambda i, f: (i, 0)),
            pl.BlockSpec((1, d), lambda i, f: (0, 0)),
            pl.BlockSpec((1, d), lambda i, f: (0, 0)),
            pl.BlockSpec((d, FFN_TF), lambda i, f: (0, f)),
            pl.BlockSpec((d, FFN_TF), lambda i, f: (0, f)),
            pl.BlockSpec((FFN_TF, d), lambda i, f: (f, 0)),
        ],
        out_specs=pl.BlockSpec((tm, d), lambda i, f: (i, 0)),
        out_shape=jax.ShapeDtypeStruct((m, d), F32),
        scratch_shapes=[pltpu.VMEM((tm, d), BF16), pltpu.VMEM((tm, d), F32)],
        compiler_params=_cparams(("parallel", "arbitrary")),
        name="ffn",
    )(x, g_pre.reshape(1, d), g_post.reshape(1, d), wg, wu, wd)


def _proj_kernel(x_ref, g_ref, wq_ref, wkv_ref, *rest, has_gates):
    if has_gates:
        wg_ref, q_ref, kv_ref, kvb_ref, gate_ref = rest
    else:
        q_ref, kv_ref, kvb_ref = rest
    xn = _rms(x_ref[...], g_ref[...]).astype(BF16)
    q = jnp.dot(xn, wq_ref[...], preferred_element_type=F32)
    q_ref[...] = (q * ATTN_SCALE).astype(BF16)
    kv = jnp.dot(xn, wkv_ref[...], preferred_element_type=F32)
    kv_ref[...] = kv
    kvb_ref[...] = kv.astype(BF16)
    if has_gates:
        gate_ref[...] = jax.nn.sigmoid(jnp.dot(xn, wg_ref[...], preferred_element_type=F32))


def norm_proj(x, g, wq, wkv, wgate=None):
    m, d = x.shape
    kvw = wkv.shape[1]
    tm = min(PROJ_TM, m)
    has_gates = wgate is not None
    row = lambda i: (i, 0)
    fixed = lambda i: (0, 0)
    in_specs = [pl.BlockSpec((tm, d), row), pl.BlockSpec((1, d), fixed),
                pl.BlockSpec((d, Q_WIDTH), fixed), pl.BlockSpec((d, kvw), fixed)]
    out_specs = [pl.BlockSpec((tm, Q_WIDTH), row), pl.BlockSpec((tm, kvw), row), pl.BlockSpec((tm, kvw), row)]
    out_shape = [jax.ShapeDtypeStruct((m, Q_WIDTH), BF16), jax.ShapeDtypeStruct((m, kvw), F32),
                 jax.ShapeDtypeStruct((m, kvw), BF16)]
    args = [x, g.reshape(1, d), wq, wkv]
    if has_gates:
        in_specs.append(pl.BlockSpec((d, LANES), fixed))
        out_specs.append(pl.BlockSpec((tm, LANES), row))
        out_shape.append(jax.ShapeDtypeStruct((m, LANES), F32))
        args.append(wgate)
    return pl.pallas_call(
        functools.partial(_proj_kernel, has_gates=has_gates),
        grid=(m // tm,), in_specs=in_specs, out_specs=out_specs, out_shape=out_shape,
        compiler_params=_cparams(("parallel",)), name="norm_proj",
    )(*args)


def _out_kernel(*refs, n_o, gated):
    o_refs = refs[:n_o]
    pos = n_o
    if gated:
        g_refs = refs[pos:pos + n_o]
        pos += n_o
    w_ref, gp_ref, x_ref, y_ref = refs[pos:pos + 4]
    o = None
    for i in range(n_o):
        t = o_refs[i][...]
        if gated:
            t = g_refs[i][...] * t
        o = t if o is None else o + t
    mix = jnp.dot(o.astype(BF16), w_ref[...], preferred_element_type=F32)
    y_ref[...] = x_ref[...] + _rms(mix, gp_ref[...])


def out_proj(o_list, w_out, g_post, x, gate_list=None):
    m, d = x.shape
    tm = min(PROJ_TM, m)
    n_o = len(o_list)
    gated = gate_list is not None
    row = lambda i: (i, 0)
    fixed = lambda i: (0, 0)
    in_specs = [pl.BlockSpec((tm, Q_WIDTH), row)] * (n_o * (2 if gated else 1))
    in_specs += [pl.BlockSpec((Q_WIDTH, d), fixed), pl.BlockSpec((1, d), fixed), pl.BlockSpec((tm, d), row)]
    args = list(o_list) + (list(gate_list) if gated else []) + [w_out, g_post.reshape(1, d), x]
    return pl.pallas_call(
        functools.partial(_out_kernel, n_o=n_o, gated=gated),
        grid=(m // tm,), in_specs=in_specs, out_specs=pl.BlockSpec((tm, d), row),
        out_shape=jax.ShapeDtypeStruct((m, d), F32),
        compiler_params=_cparams(("parallel",)), name="out_proj",
    )(*args)


def _col(ref, heads, rows):
    return jnp.concatenate([jnp.full((rows, 1), ref[h], F32) for h in heads], axis=0)


def _expand_blocks(sel, key0, tk, bs):
    nb = sel.shape[1]
    blk = (key0 + lax.broadcasted_iota(I32, (nb, tk), 1)) // bs
    onehot = (blk == lax.broadcasted_iota(I32, (nb, tk), 0)).astype(BF16)
    return jnp.dot(sel, onehot, preferred_element_type=F32)


def _topk_mask(sc, nb, topk):
    jidx = lax.broadcasted_iota(I32, sc.shape, 1)
    rank = jnp.zeros(sc.shape, I32)
    for jp in range(nb):
        col = sc[:, jp:jp + 1]
        ahead = (col > sc) | ((col == sc) & (jidx > jp))
        rank = rank + ahead.astype(I32)
    return rank < topk


def _flash(qg, k_at, v_at, lo, hi, tk, qpos, slope, mask_fn, m0, l0, m_ref, l_ref, acc_ref):
    m_ref[...] = m0
    l_ref[...] = l0
    acc_ref[...] = jnp.zeros_like(acc_ref)

    def body(kt, carry):
        s = _mm_nt(qg, k_at(kt))
        kpos = kt * tk + lax.broadcasted_iota(I32, (1, tk), 1)
        dist = qpos - kpos
        s = jnp.where(mask_fn(kt, dist), s - slope * dist.astype(F32), -jnp.inf)
        m_old = m_ref[...]
        m_new = jnp.maximum(m_old, jnp.max(s, axis=1, keepdims=True))
        alpha = jnp.exp(m_old - m_new)
        p = jnp.exp(s - m_new)
        l_ref[...] = alpha * l_ref[...] + jnp.sum(p, axis=1, keepdims=True)
        acc_ref[...] = alpha * acc_ref[...] + jnp.dot(p.astype(BF16), v_at(kt), preferred_element_type=F32)
        m_ref[...] = m_new
        return carry

    lax.fori_loop(lo, hi, body, 0)
    return acc_ref[...] / jnp.maximum(l_ref[...], TINY)


def _softmax_rows(s, sink=None):
    m = jnp.max(s, axis=1, keepdims=True)
    if sink is not None:
        m = jnp.maximum(m, sink)
    m = jnp.where(m > -jnp.inf, m, 0.0)
    e = jnp.exp(s - m)
    den = jnp.sum(e, axis=1, keepdims=True)
    if sink is not None:
        den = den + jnp.exp(sink - m)
    return e / jnp.maximum(den, TINY)


def _gelu(x):
    return jax.nn.gelu(x)


def _compress_rows(load_rows, nch, wbig_ref, pos_ref, w2_ref):
    acc = jnp.zeros((nch, 4 * CMP_HID), F32)
    for l in range(CMP_STRIDE):
        acc = acc + jnp.dot(load_rows(l).astype(BF16), wbig_ref[l], preferred_element_type=F32)
    first = acc[:, :2 * CMP_HID]
    second = pltpu.roll(acc[:, 2 * CMP_HID:], nch - 1, 0)
    hid = first + second + pos_ref[0:1, :]
    return jnp.dot(_gelu(hid).astype(BF16), w2_ref[...], preferred_element_type=F32)


def _compress_prompt_kernel(k_ref, v_ref, wk_ref, pk_ref, w2k_ref, wv_ref, pv_ref, w2v_ref, ok_ref, ov_ref, *, nch):
    for x_ref, w, p, w2, o_ref in ((k_ref, wk_ref, pk_ref, w2k_ref, ok_ref), (v_ref, wv_ref, pv_ref, w2v_ref, ov_ref)):
        load = lambda l, x_ref=x_ref: x_ref[0, pl.ds(l, nch, stride=CMP_STRIDE), :]
        o_ref[0] = _compress_rows(load, nch, w, p, w2).astype(BF16)


def compress_prompt(kv, cw):
    n, t, _ = kv.shape
    nch = t // CMP_STRIDE
    wspec = [pl.BlockSpec((CMP_STRIDE, LANES, 4 * CMP_HID), lambda i: (0, 0, 0)),
             pl.BlockSpec((8, 2 * CMP_HID), lambda i: (0, 0)),
             pl.BlockSpec((2 * CMP_HID, LANES), lambda i: (0, 0))]
    out = jax.ShapeDtypeStruct((n, nch, LANES), BF16)
    return pl.pallas_call(
        functools.partial(_compress_prompt_kernel, nch=nch),
        grid=(n,),
        in_specs=[pl.BlockSpec((1, t, LANES), lambda i: (i, 0, 0)),
                  pl.BlockSpec((1, t, LANES), lambda i: (i, 0, 1))] + wspec + wspec,
        out_specs=[pl.BlockSpec((1, nch, LANES), lambda i: (i, 0, 0))] * 2,
        out_shape=[out, out],
        compiler_params=_cparams(("parallel",)), name="compress_prompt",
    )(kv, kv, *cw["k"], *cw["v"])


def _pos_term_kernel(p_ref, w_ref, o_ref):
    o_ref[...] = jnp.dot(p_ref[...], w_ref[...], precision=lax.Precision.HIGHEST, preferred_element_type=F32)


def _compress_weights(pos, w1, w2):
    eye = jnp.eye(KV_A, dtype=F32)
    w1r = w1.reshape(CMP_LEN // CMP_STRIDE, CMP_STRIDE, HEAD_DIM, CMP_HID)
    wbig = jnp.einsum("jldh,ab->lbdjah", w1r, eye).reshape(CMP_STRIDE, LANES, 4 * CMP_HID).astype(BF16)
    w2big = jnp.einsum("hd,ab->ahbd", w2, eye).reshape(KV_A * CMP_HID, LANES).astype(BF16)
    pos8 = jnp.broadcast_to(pos.reshape(1, CMP_LEN * HEAD_DIM), (8, CMP_LEN * HEAD_DIM))
    pterm = pl.pallas_call(
        _pos_term_kernel, out_shape=jax.ShapeDtypeStruct((8, CMP_HID), F32), name="cmp_pos_term",
    )(pos8, w1)
    return wbig, jnp.concatenate([pterm, pterm], axis=1), w2big


def _page_copies(ptab_ref, pools, bufs, sem, n, c, slot, pps):
    copies = []
    for i in range(pps):
        page = ptab_ref[n, c * pps + i]
        for j, (pool, buf) in enumerate(zip(pools, bufs)):
            copies.append(pltpu.make_async_copy(pool.at[page], buf.at[slot, pl.ds(i * PAGE, PAGE)], sem.at[j, slot]))
    return copies


def _paged_pipeline(ptab_ref, pools, bufs, sem, nc, pps):
    n, c = pl.program_id(0), pl.program_id(1)
    step = n * nc + c
    slot = step % 2

    @pl.when(step == 0)
    def _():
        for cp in _page_copies(ptab_ref, pools, bufs, sem, n, c, slot, pps):
            cp.start()

    @pl.when(step + 1 < pl.num_programs(0) * nc)
    def _():
        nxt = step + 1
        for cp in _page_copies(ptab_ref, pools, bufs, sem, nxt // nc, nxt % nc, 1 - slot, pps):
            cp.start()

    for cp in _page_copies(ptab_ref, pools, bufs, sem, n, c, slot, pps):
        cp.wait()
    return slot


def _compress_sample_kernel(ptab_ref, kpool, vpool, wk_ref, pk_ref, w2k_ref, wv_ref, pv_ref, w2v_ref,
                            ok_ref, ov_ref, kbuf, vbuf, sem, *, n_pages):
    slot = _paged_pipeline(ptab_ref, (kpool, vpool), (kbuf, vbuf), sem, 1, n_pages)
    nch = n_pages * PAGE // CMP_STRIDE
    for buf, w, p, w2, o_ref in ((kbuf, wk_ref, pk_ref, w2k_ref, ok_ref), (vbuf, wv_ref, pv_ref, w2v_ref, ov_ref)):
        load = lambda l, buf=buf: buf[slot, pl.ds(l, nch, stride=CMP_STRIDE), :]
        o_ref[0] = _compress_rows(load, nch, w, p, w2).astype(BF16)


def compress_sample(ptab, pool_k, pool_v, cw):
    n, n_pages = ptab.shape
    rows = n_pages * PAGE
    nch = rows // CMP_STRIDE
    wspec = [pl.BlockSpec((CMP_STRIDE, LANES, 4 * CMP_HID), lambda i, c, pt: (0, 0, 0)),
             pl.BlockSpec((8, 2 * CMP_HID), lambda i, c, pt: (0, 0)),
             pl.BlockSpec((2 * CMP_HID, LANES), lambda i, c, pt: (0, 0))]
    anyspec = pl.BlockSpec(memory_space=pl.ANY)
    out = jax.ShapeDtypeStruct((n, nch, LANES), BF16)
    return pl.pallas_call(
        functools.partial(_compress_sample_kernel, n_pages=n_pages),
        grid_spec=pltpu.PrefetchScalarGridSpec(
            num_scalar_prefetch=1, grid=(n, 1),
            in_specs=[anyspec, anyspec] + wspec + wspec,
            out_specs=[pl.BlockSpec((1, nch, LANES), lambda i, c, pt: (i, 0, 0))] * 2,
            scratch_shapes=[pltpu.VMEM((2, rows, LANES), F32), pltpu.VMEM((2, rows, LANES), F32),
                            pltpu.SemaphoreType.DMA((2, 2))]),
        out_shape=[out, out],
        compiler_params=_cparams(("arbitrary", "arbitrary")), name="compress_sample",
    )(ptab, pool_k, pool_v, *cw["k"], *cw["v"])


def _nsa_prompt_kernel(slopes_ref, q_ref, kv_ref, kc_ref, vc_ref, g_ref, o_ref, m_ref, l_ref, acc_ref, *, t_len):
    qi = pl.program_id(1)
    q0 = qi * TQ
    hg = N_HEADS // KV_A
    rows = hg * TQ
    qpos = q0 + (lax.broadcasted_iota(I32, (rows, 1), 0) & (TQ - 1))
    qpos_t = q0 + lax.broadcasted_iota(I32, (TQ, 1), 0)
    gates = g_ref[0]
    ncp = kc_ref.shape[2]
    n_cmp = t_len // CMP_STRIDE - CMP_LEN // CMP_STRIDE + 1
    n_sel = t_len // SEL_BLOCK
    m0 = jnp.full((rows, 1), M_INIT, F32)
    l0 = jnp.zeros((rows, 1), F32)

    for g in range(KV_A):
        heads = range(g * hg, (g + 1) * hg)
        qg = q_ref[0, g * hg:(g + 1) * hg].reshape(rows, HEAD_DIM)
        slope = _col(slopes_ref, heads, TQ)

        cidx = lax.broadcasted_iota(I32, (1, ncp), 1)
        dist = qpos - (cidx * CMP_STRIDE + CMP_LEN - 1)
        s = _mm_nt(qg, kc_ref[0, g])
        s = jnp.where((dist >= 0) & (cidx < n_cmp), s - slope * dist.astype(F32), -jnp.inf)
        p = _softmax_rows(s)
        o_c = jnp.dot(p.astype(BF16), vc_ref[0, g], preferred_element_type=F32)
        imp = p[0:TQ]
        for h in range(1, hg):
            imp = imp + p[h * TQ:(h + 1) * TQ]

        crow = lax.broadcasted_iota(I32, (ncp, LANES), 0) * CMP_STRIDE
        scol = lax.broadcasted_iota(I32, (ncp, LANES), 1) * SEL_BLOCK
        overlap = ((crow < scol + SEL_BLOCK) & (crow + CMP_LEN > scol)).astype(F32)
        score = jnp.dot(imp, overlap, precision=lax.Precision.HIGHEST, preferred_element_type=F32)
        jidx = lax.broadcasted_iota(I32, (1, LANES), 1)
        qb = qpos_t // SEL_BLOCK
        forced = (jidx == 0) | (jidx == qb) | (jidx == qb - 1)
        causal = jidx <= qb
        sc = jnp.where(forced, jnp.inf, jnp.where(causal, score, -jnp.inf))
        sel = (_topk_mask(sc, n_sel, SEL_TOPK) & causal).astype(BF16)

        def sel_mask(kt, dist, sel=sel):
            e = _expand_blocks(sel, kt * TQ, TQ, SEL_BLOCK)
            e = jnp.concatenate([e] * hg, axis=0)
            return (dist >= 0) & (e > 0.5)

        o_s = _flash(qg, lambda kt, g=g: kv_ref[0, 4 + g, pl.ds(pl.multiple_of(kt * TQ, TQ), TQ), :],
                     lambda kt, g=g: kv_ref[0, 6 + g, pl.ds(pl.multiple_of(kt * TQ, TQ), TQ), :],
                     0, qi + 1, TQ, qpos, slope, sel_mask, m0, l0, m_ref, l_ref, acc_ref)

        win_mask = lambda kt, dist: (dist >= 0) & (dist < WIN_A)
        lo = jnp.maximum(qi - (-(-(WIN_A - 1) // TQ)), 0)
        o_w = _flash(qg, lambda kt, g=g: kv_ref[0, 8 + g, pl.ds(pl.multiple_of(kt * TQ, TQ), TQ), :],
                     lambda kt, g=g: kv_ref[0, 10 + g, pl.ds(pl.multiple_of(kt * TQ, TQ), TQ), :],
                     lo, qi + 1, TQ, qpos, slope, win_mask, m0, l0, m_ref, l_ref, acc_ref)

        gcol = lambda br: jnp.concatenate([gates[:, br * N_HEADS + h:br * N_HEADS + h + 1] for h in heads], axis=0)
        o = gcol(0) * o_c + gcol(1) * o_s + gcol(2) * o_w
        o_ref[0, g * hg:(g + 1) * hg] = o.reshape(hg, TQ, HEAD_DIM)


def nsa_prompt_attn(slopes, q_hm, kv_hm, kc_hm, vc_hm, gates):
    n, _, t, _ = q_hm.shape
    ncp = kc_hm.shape[2]
    rows = (N_HEADS // KV_A) * TQ
    return pl.pallas_call(
        functools.partial(_nsa_prompt_kernel, t_len=t),
        grid_spec=pltpu.PrefetchScalarGridSpec(
            num_scalar_prefetch=1, grid=(n, t // TQ),
            in_specs=[pl.BlockSpec((1, N_HEADS, TQ, HEAD_DIM), lambda i, j, s: (i, 0, j, 0)),
                      pl.BlockSpec((1, 12, t, HEAD_DIM), lambda i, j, s: (i, 0, 0, 0)),
                      pl.BlockSpec((1, KV_A, ncp, HEAD_DIM), lambda i, j, s: (i, 0, 0, 0)),
                      pl.BlockSpec((1, KV_A, ncp, HEAD_DIM), lambda i, j, s: (i, 0, 0, 0)),
                      pl.BlockSpec((1, TQ, LANES), lambda i, j, s: (i, j, 0))],
            out_specs=pl.BlockSpec((1, N_HEADS, TQ, HEAD_DIM), lambda i, j, s: (i, 0, j, 0)),
            scratch_shapes=[pltpu.VMEM((rows, 1), F32), pltpu.VMEM((rows, 1), F32),
                            pltpu.VMEM((rows, HEAD_DIM), F32)]),
        out_shape=jax.ShapeDtypeStruct((n, N_HEADS, t, HEAD_DIM), F32),
        compiler_params=_cparams(("parallel", "arbitrary")), name="nsa_prompt_attn",
    )(slopes, q_hm, kv_hm, kc_hm, vc_hm, gates)


def _swa_prompt_kernel(slopes_ref, sinks_ref, q_ref, kv_ref, o_ref, m_ref, l_ref, acc_ref):
    qi = pl.program_id(1)
    hg = N_HEADS // KV_B
    rows = hg * TQ
    qpos = qi * TQ + (lax.broadcasted_iota(I32, (rows, 1), 0) & (TQ - 1))
    win_mask = lambda kt, dist: (dist >= 0) & (dist < WIN_B)
    lo = jnp.maximum(qi - (-(-(WIN_B - 1) // TQ)), 0)
    for g in range(KV_B):
        heads = range(g * hg, (g + 1) * hg)
        qg = q_ref[0, g * hg:(g + 1) * hg].reshape(rows, HEAD_DIM)
        o = _flash(qg, lambda kt, g=g: kv_ref[0, g, pl.ds(pl.multiple_of(kt * TQ, TQ), TQ), :],
                   lambda kt, g=g: kv_ref[0, KV_B + g, pl.ds(pl.multiple_of(kt * TQ, TQ), TQ), :],
                   lo, qi + 1, TQ, qpos, _col(slopes_ref, heads, TQ), win_mask,
                   _col(sinks_ref, heads, TQ), jnp.ones((rows, 1), F32), m_ref, l_ref, acc_ref)
        o_ref[0, g * hg:(g + 1) * hg] = o.reshape(hg, TQ, HEAD_DIM)


def swa_prompt_attn(slopes, sinks, q_hm, kv_hm):
    n, _, t, _ = q_hm.shape
    rows = (N_HEADS // KV_B) * TQ
    return pl.pallas_call(
        _swa_prompt_kernel,
        grid_spec=pltpu.PrefetchScalarGridSpec(
            num_scalar_prefetch=2, grid=(n, t // TQ),
            in_specs=[pl.BlockSpec((1, N_HEADS, TQ, HEAD_DIM), lambda i, j, s, k: (i, 0, j, 0)),
                      pl.BlockSpec((1, 2 * KV_B, t, HEAD_DIM), lambda i, j, s, k: (i, 0, 0, 0))],
            out_specs=pl.BlockSpec((1, N_HEADS, TQ, HEAD_DIM), lambda i, j, s, k: (i, 0, j, 0)),
            scratch_shapes=[pltpu.VMEM((rows, 1), F32), pltpu.VMEM((rows, 1), F32),
                            pltpu.VMEM((rows, HEAD_DIM), F32)]),
        out_shape=jax.ShapeDtypeStruct((n, N_HEADS, t, HEAD_DIM), F32),
        compiler_params=_cparams(("parallel", "arbitrary")), name="swa_prompt_attn",
    )(slopes, sinks, q_hm, kv_hm)


def _block_mean_prompt_kernel(k_ref, o_ref, *, nb):
    for b in range(nb):
        blk = k_ref[0, b * MOBA_BLOCK:(b + 1) * MOBA_BLOCK, :]
        o_ref[0, b:b + 1, :] = jnp.sum(blk, axis=0, keepdims=True) / MOBA_BLOCK


def block_means_prompt(kv):
    n, t, _ = kv.shape
    nb = t // MOBA_BLOCK
    w = KV_C * HEAD_DIM
    return pl.pallas_call(
        functools.partial(_block_mean_prompt_kernel, nb=nb),
        grid=(n,),
        in_specs=[pl.BlockSpec((1, t, w), lambda i: (i, 0, 0))],
        out_specs=pl.BlockSpec((1, nb, w), lambda i: (i, 0, 0)),
        out_shape=jax.ShapeDtypeStruct((n, nb, w), F32),
        compiler_params=_cparams(("parallel",)), name="block_means_prompt",
    )(kv)


def _moba_prompt_kernel(slopes_ref, q_ref, kv_ref, kmh_ref, kml_ref, o_ref, m_ref, l_ref, acc_ref, *, t_len):
    qi = pl.program_id(1)
    q0 = qi * TQ
    hg = N_HEADS // KV_C
    rows = hg * TQ
    tk = MOBA_BLOCK
    nb = t_len // MOBA_BLOCK
    qpos = q0 + (lax.broadcasted_iota(I32, (rows, 1), 0) & (TQ - 1))
    m0 = jnp.full((rows, 1), M_INIT, F32)
    l0 = jnp.zeros((rows, 1), F32)
    bidx = lax.broadcasted_iota(I32, (1, LANES), 1)
    qb = qpos // MOBA_BLOCK
    for g in range(KV_C):
        heads = range(g * hg, (g + 1) * hg)
        qg = q_ref[0, g * hg:(g + 1) * hg].reshape(rows, HEAD_DIM)
        gate = _mm_nt(qg, kmh_ref[0, g]) + _mm_nt(qg, kml_ref[0, g])
        past_blk = bidx < qb
        sc = jnp.where(past_blk, gate, -jnp.inf)
        sel = ((_topk_mask(sc, nb, MOBA_TOPK) & past_blk) | (bidx == qb)).astype(BF16)

        def blk_mask(kt, dist, sel=sel):
            return (dist >= 0) & (_expand_blocks(sel, kt * tk, tk, MOBA_BLOCK) > 0.5)

        o = _flash(qg, lambda kt, g=g: kv_ref[0, g, pl.ds(pl.multiple_of(kt * tk, tk), tk), :],
                   lambda kt, g=g: kv_ref[0, KV_C + g, pl.ds(pl.multiple_of(kt * tk, tk), tk), :],
                   0, (q0 + TQ - 1) // tk + 1, tk, qpos, _col(slopes_ref, heads, TQ), blk_mask,
                   m0, l0, m_ref, l_ref, acc_ref)
        o_ref[0, g * hg:(g + 1) * hg] = o.reshape(hg, TQ, HEAD_DIM)


def moba_prompt_attn(slopes, q_hm, kv_hm, kmh, kml):
    n, _, t, _ = q_hm.shape
    rows = (N_HEADS // KV_C) * TQ
    return pl.pallas_call(
        functools.partial(_moba_prompt_kernel, t_len=t),
        grid_spec=pltpu.PrefetchScalarGridSpec(
            num_scalar_prefetch=1, grid=(n, t // TQ),
            in_specs=[pl.BlockSpec((1, N_HEADS, TQ, HEAD_DIM), lambda i, j, s: (i, 0, j, 0)),
                      pl.BlockSpec((1, 2 * KV_C, t, HEAD_DIM), lambda i, j, s: (i, 0, 0, 0)),
                      pl.BlockSpec((1, KV_C, LANES, HEAD_DIM), lambda i, j, s: (i, 0, 0, 0)),
                      pl.BlockSpec((1, KV_C, LANES, HEAD_DIM), lambda i, j, s: (i, 0, 0, 0))],
            out_specs=pl.BlockSpec((1, N_HEADS, TQ, HEAD_DIM), lambda i, j, s: (i, 0, j, 0)),
            scratch_shapes=[pltpu.VMEM((rows, 1), F32), pltpu.VMEM((rows, 1), F32),
                            pltpu.VMEM((rows, HEAD_DIM), F32)]),
        out_shape=jax.ShapeDtypeStruct((n, N_HEADS, t, HEAD_DIM), F32),
        compiler_params=_cparams(("parallel", "arbitrary")), name="moba_prompt_attn",
    )(slopes, q_hm, kv_hm, kmh, kml)


def _sample_cols(slopes_ref, s_len, past):
    slope = _col(slopes_ref, range(N_HEADS), s_len)
    qpos = past + (lax.broadcasted_iota(I32, (N_HEADS * s_len, 1), 0) % s_len)
    return slope, qpos


def _window_sample(qx, k_all, v_all, qpos, slope, wb, window, past, sink=None):
    kpos = past - wb + lax.broadcasted_iota(I32, (1, k_all.shape[0]), 1)
    dist = qpos - kpos
    s = _mm_nt(qx, k_all.astype(BF16))
    s = jnp.where((dist >= 0) & (dist < window) & (kpos >= 0), s - slope * dist.astype(F32), -jnp.inf)
    p = _softmax_rows(s, sink)
    return jnp.dot(p.astype(BF16), v_all.astype(BF16), preferred_element_type=F32)


def _nsa_sample_small_kernel(slopes_ref, qx_ref, kc_ref, vc_ref, kw_ref, vw_ref, oc_ref, ow_ref, sel_ref,
                             *, s_len, past, wb):
    hg = N_HEADS // KV_A
    slope, qpos = _sample_cols(slopes_ref, s_len, past)
    qx = qx_ref[0]
    ncp = kc_ref.shape[1]
    n_cmp = past // CMP_STRIDE - CMP_LEN // CMP_STRIDE + 1
    n_sel = -(-(past + s_len) // SEL_BLOCK)
    nbl = sel_ref.shape[2]

    cidx = lax.broadcasted_iota(I32, (1, ncp), 1)
    dist = qpos - (cidx * CMP_STRIDE + CMP_LEN - 1)
    s = _mm_nt(qx, kc_ref[0])
    s = jnp.where((dist >= 0) & (cidx < n_cmp), s - slope * dist.astype(F32), -jnp.inf)
    p = _softmax_rows(s)
    oc_ref[0] = jnp.dot(p.astype(BF16), vc_ref[0], preferred_element_type=F32)

    imps = []
    for g in range(KV_A):
        base = g * hg * s_len
        imp = p[base:base + s_len]
        for h in range(1, hg):
            imp = imp + p[base + h * s_len:base + (h + 1) * s_len]
        imps.append(imp)
    imp = jnp.concatenate(imps, axis=0)
    crow = lax.broadcasted_iota(I32, (ncp, nbl), 0) * CMP_STRIDE
    scol = lax.broadcasted_iota(I32, (ncp, nbl), 1) * SEL_BLOCK
    overlap = ((crow < scol + SEL_BLOCK) & (crow + CMP_LEN > scol)
               & (crow < n_cmp * CMP_STRIDE)).astype(F32)
    score = jnp.dot(imp, overlap, precision=lax.Precision.HIGHEST, preferred_element_type=F32)
    jidx = lax.broadcasted_iota(I32, (1, nbl), 1)
    tpos = past + (lax.broadcasted_iota(I32, (KV_A * s_len, 1), 0) % s_len)
    qb = tpos // SEL_BLOCK
    forced = (jidx == 0) | (jidx == qb) | (jidx == qb - 1)
    causal = jidx <= qb
    sc = jnp.where(forced, jnp.inf, jnp.where(causal, score, -jnp.inf))
    sel = (_topk_mask(sc, n_sel, SEL_TOPK) & causal).astype(F32)
    sel_ref[0] = jnp.concatenate([sel[g * s_len:(g + 1) * s_len] for g in range(KV_A) for _ in range(hg)], axis=0)

    ow_ref[0] = _window_sample(qx, kw_ref[0], vw_ref[0], qpos, slope, wb, WIN_A, past)


def nsa_sample_small(slopes, qx, kcmp, vcmp, kw_all, vw_all, past, nbl):
    n, rows, w = qx.shape
    s_len = rows // N_HEADS
    ncp = kcmp.shape[1]
    lw = kw_all.shape[1]
    per_seq = lambda shape: pl.BlockSpec((1,) + shape, lambda i, s: (i, 0, 0))
    return pl.pallas_call(
        functools.partial(_nsa_sample_small_kernel, s_len=s_len, past=past, wb=lw - s_len),
        grid_spec=pltpu.PrefetchScalarGridSpec(
            num_scalar_prefetch=1, grid=(n,),
            in_specs=[per_seq((rows, w)), per_seq((ncp, w)), per_seq((ncp, w)), per_seq((lw, w)), per_seq((lw, w))],
            out_specs=[per_seq((rows, w)), per_seq((rows, w)), per_seq((rows, nbl))]),
        out_shape=[jax.ShapeDtypeStruct((n, rows, w), F32), jax.ShapeDtypeStruct((n, rows, w), F32),
                   jax.ShapeDtypeStruct((n, rows, nbl), F32)],
        compiler_params=_cparams(("parallel",)), name="nsa_sample_small",
    )(slopes, qx, kcmp, vcmp, kw_all, vw_all)


def _swa_sample_kernel(slopes_ref, sinks_ref, qx_ref, k_ref, v_ref, o_ref, *, s_len, past, wb):
    slope, qpos = _sample_cols(slopes_ref, s_len, past)
    sink = _col(sinks_ref, range(N_HEADS), s_len)
    o_ref[0] = _window_sample(qx_ref[0], k_ref[0], v_ref[0], qpos, slope, wb, WIN_B, past, sink)


def swa_sample_attn(slopes, sinks, qx, k_all, v_all, past):
    n, rows, w = qx.shape
    s_len = rows // N_HEADS
    lw = k_all.shape[1]
    per_seq = lambda shape: pl.BlockSpec((1,) + shape, lambda i, s, k: (i, 0, 0))
    return pl.pallas_call(
        functools.partial(_swa_sample_kernel, s_len=s_len, past=past, wb=lw - s_len),
        grid_spec=pltpu.PrefetchScalarGridSpec(
            num_scalar_prefetch=2, grid=(n,),
            in_specs=[per_seq((rows, w)), per_seq((lw, w)), per_seq((lw, w))],
            out_specs=per_seq((rows, w))),
        out_shape=jax.ShapeDtypeStruct((n, rows, w), F32),
        compiler_params=_cparams(("parallel",)), name="swa_sample_attn",
    )(slopes, sinks, qx, k_all, v_all)


def _paged_attn_kernel(ptab_ref, slopes_ref, qx_ref, sel_ref, knew_ref, vnew_ref, kpool, vpool, o_ref,
                       kbuf, vbuf, sem, m_ref, l_ref, acc_ref, selb_ref,
                       *, nc, pps, bs, tks, s_len, past, from_means):
    c = pl.program_id(1)
    slot = _paged_pipeline(ptab_ref, (kpool, vpool), (kbuf, vbuf), sem, nc, pps)
    slope, qpos = _sample_cols(slopes_ref, s_len, past)
    qx = qx_ref[0]
    rows = qx.shape[0]

    @pl.when(c == 0)
    def _():
        m_ref[...] = jnp.full_like(m_ref, M_INIT)
        l_ref[...] = jnp.zeros_like(l_ref)
        acc_ref[...] = jnp.zeros_like(acc_ref)
        if from_means:
            km = sel_ref[0]
            kmh = km.astype(BF16)
            kml = (km - kmh.astype(F32)).astype(BF16)
            gate = _mm_nt(qx, kmh) + _mm_nt(qx, kml)
            bidx = lax.broadcasted_iota(I32, (1, km.shape[0]), 1)
            past_blk = bidx < qpos // bs
            sc = jnp.where(past_blk, gate, -jnp.inf)
            selb_ref[...] = (_topk_mask(sc, past // bs, MOBA_TOPK) & past_blk).astype(BF16)
        else:
            selb_ref[...] = sel_ref[0].astype(BF16)

    def update(s, v):
        m_old = m_ref[...]
        m_new = jnp.maximum(m_old, jnp.max(s, axis=1, keepdims=True))
        alpha = jnp.exp(m_old - m_new)
        p = jnp.exp(s - m_new)
        l_ref[...] = alpha * l_ref[...] + jnp.sum(p, axis=1, keepdims=True)
        acc_ref[...] = alpha * acc_ref[...] + jnp.dot(p.astype(BF16), v, preferred_element_type=F32)
        m_ref[...] = m_new

    def tile(t, carry):
        r0 = pl.multiple_of(t * tks, tks)
        k = kbuf[slot, pl.ds(r0, tks), :].astype(BF16)
        v = vbuf[slot, pl.ds(r0, tks), :].astype(BF16)
        key0 = c * (pps * PAGE) + t * tks
        dist = qpos - (key0 + lax.broadcasted_iota(I32, (1, tks), 1))
        keep = _expand_blocks(selb_ref[...], key0, tks, bs) > 0.5
        s = jnp.where(keep, _mm_nt(qx, k) - slope * dist.astype(F32), -jnp.inf)
        update(s, v)
        return carry

    lax.fori_loop(0, pps * PAGE // tks, tile, 0)

    @pl.when(c == nc - 1)
    def _():
        j = lax.broadcasted_iota(I32, (1, knew_ref.shape[1]), 1)
        dist = qpos - (past + j)
        s = _mm_nt(qx, knew_ref[0].astype(BF16))
        s = jnp.where((dist >= 0) & (j < s_len), s - slope * dist.astype(F32), -jnp.inf)
        update(s, vnew_ref[0].astype(BF16))
        o_ref[0] = acc_ref[...] / jnp.maximum(l_ref[...], TINY)


def paged_attn(ptab, slopes, qx, sel, knew, vnew, pool_k, pool_v, *, bs, nc, tks, past, from_means):
    n, n_pages = ptab.shape
    _, rows, w = qx.shape
    s_len = rows // N_HEADS
    pps = n_pages // nc
    nbl = LANES if from_means else sel.shape[2]
    per_seq = lambda shape: pl.BlockSpec((1,) + shape, lambda i, c, pt, sl: (i, 0, 0))
    anyspec = pl.BlockSpec(memory_space=pl.ANY)
    return pl.pallas_call(
        functools.partial(_paged_attn_kernel, nc=nc, pps=pps, bs=bs, tks=tks, s_len=s_len, past=past,
                          from_means=from_means),
        grid_spec=pltpu.PrefetchScalarGridSpec(
            num_scalar_prefetch=2, grid=(n, nc),
            in_specs=[per_seq((rows, w)), per_seq(sel.shape[1:]), per_seq(knew.shape[1:]), per_seq(vnew.shape[1:]),
                      anyspec, anyspec],
            out_specs=per_seq((rows, w)),
            scratch_shapes=[pltpu.VMEM((2, pps * PAGE, w), F32), pltpu.VMEM((2, pps * PAGE, w), F32),
                            pltpu.SemaphoreType.DMA((2, 2)),
                            pltpu.VMEM((rows, 1), F32), pltpu.VMEM((rows, 1), F32), pltpu.VMEM((rows, w), F32),
                            pltpu.VMEM((rows, nbl), BF16)]),
        out_shape=jax.ShapeDtypeStruct((n, rows, w), F32),
        compiler_params=_cparams(("arbitrary", "arbitrary")), name="paged_attn",
    )(ptab, slopes, qx, sel, knew, vnew, pool_k, pool_v)


def _block_mean_sample_kernel(ptab_ref, kpool, o_ref, kbuf, sem, *, nc, pps):
    slot = _paged_pipeline(ptab_ref, (kpool,), (kbuf,), sem, nc, pps)
    for b in range(pps * PAGE // MOBA_BLOCK):
        blk = kbuf[slot, b * MOBA_BLOCK:(b + 1) * MOBA_BLOCK, :]
        o_ref[0, b:b + 1, :] = jnp.sum(blk, axis=0, keepdims=True) / MOBA_BLOCK


def block_means_sample(ptab, pool_k, nc):
    n, n_pages = ptab.shape
    w = pool_k.shape[2]
    pps = n_pages // nc
    nbc = pps * PAGE // MOBA_BLOCK
    return pl.pallas_call(
        functools.partial(_block_mean_sample_kernel, nc=nc, pps=pps),
        grid_spec=pltpu.PrefetchScalarGridSpec(
            num_scalar_prefetch=1, grid=(n, nc),
            in_specs=[pl.BlockSpec(memory_space=pl.ANY)],
            out_specs=pl.BlockSpec((1, nbc, w), lambda i, c, pt: (i, c, 0)),
            scratch_shapes=[pltpu.VMEM((2, pps * PAGE, w), F32), pltpu.SemaphoreType.DMA((1, 2))]),
        out_shape=jax.ShapeDtypeStruct((n, nc * nbc, w), F32),
        compiler_params=_cparams(("arbitrary", "arbitrary")), name="block_means_sample",
    )(ptab, pool_k)


def _head_major(x, n, t):
    c = x.shape[1] // HEAD_DIM
    return x.reshape(n, t, c, HEAD_DIM).transpose(0, 2, 1, 3)


def _token_major(o_hm):
    n, h, t, d = o_hm.shape
    return o_hm.transpose(0, 2, 1, 3).reshape(n * t, h * d)


def _stack_queries(q, n, s_len, groups):
    hg = N_HEADS // groups
    qh = q.reshape(n, s_len, groups, hg, 1, HEAD_DIM).transpose(0, 2, 3, 1, 4, 5)
    eye = jnp.eye(groups, dtype=q.dtype).reshape(1, groups, 1, 1, groups, 1)
    return (qh * eye).reshape(n, N_HEADS * s_len, groups * HEAD_DIM)


def _unstack_outputs(o, n, s_len, groups):
    hg = N_HEADS // groups
    o6 = o.reshape(n, groups, hg, s_len, groups, HEAD_DIM)
    diag = jnp.stack([o6[:, g, :, :, g, :] for g in range(groups)], axis=1)
    return diag.transpose(0, 3, 1, 2, 4).reshape(n * s_len, Q_WIDTH)


def _pad_rows(x, rows):
    return jnp.pad(x, ((0, 0), (0, rows - x.shape[1]), (0, 0)))


def _nsa_layer(hp_x, hs_x, g_pre, w_in, cw, pools, win_bufs, ptab, slopes, n_p, t, n_s, s_len):
    kvw = 6 * KV_A * HEAD_DIM
    wq = w_in[:, :Q_WIDTH].astype(BF16)
    wkv = w_in[:, Q_WIDTH:Q_WIDTH + kvw].astype(BF16)
    wgate = jnp.pad(w_in[:, Q_WIDTH + kvw:], ((0, 0), (0, LANES - 3 * N_HEADS))).astype(BF16)
    past = ptab.shape[1] * PAGE
    gw = KV_A * HEAD_DIM

    q, kv, kvb, gates = norm_proj(hp_x, g_pre, wq, wkv, wgate)
    kcmp, vcmp = compress_prompt(kv.reshape(n_p, t, kvw), cw)
    ncp = max(kcmp.shape[1], LANES)
    kc_hm = _head_major(_pad_rows(kcmp, ncp).reshape(n_p * ncp, gw), n_p, ncp)
    vc_hm = _head_major(_pad_rows(vcmp, ncp).reshape(n_p * ncp, gw), n_p, ncp)
    o_hm = nsa_prompt_attn(slopes, _head_major(q, n_p, t), _head_major(kvb, n_p, t), kc_hm, vc_hm,
                           gates.reshape(n_p, t, LANES))
    o_p = [_token_major(o_hm)]
    kv5 = kv.reshape(n_p, t, 6, KV_A, HEAD_DIM)
    wb = min(WIN_A, t)
    st_p = tuple(kv5[:, :, i] for i in range(4)) + (kv5[:, t - wb:, 4], kv5[:, t - wb:, 5])

    q, kv, kvb, gates = norm_proj(hs_x, g_pre, wq, wkv, wgate)
    kv5 = kv.reshape(n_s, s_len, 6, gw)
    pool_ck, pool_cv, pool_sk, pool_sv = [p.reshape(p.shape[0], PAGE, gw) for p in pools]
    buf_k, buf_v = [b.reshape(n_s, b.shape[1], gw) for b in win_bufs]
    kw_all = jnp.concatenate([buf_k, kv5[:, :, 4]], axis=1)
    vw_all = jnp.concatenate([buf_v, kv5[:, :, 5]], axis=1)
    kcmp, vcmp = compress_sample(ptab, pool_ck, pool_cv, cw)
    qx = _stack_queries(q, n_s, s_len, KV_A)
    nbl = -(-(-(-(past + s_len) // SEL_BLOCK)) // LANES) * LANES
    o_c, o_w, sel = nsa_sample_small(slopes, qx, kcmp, vcmp, kw_all, vw_all, past, nbl)
    o_s = paged_attn(ptab, slopes, qx, sel, _pad_rows(kv5[:, :, 2], LANES), _pad_rows(kv5[:, :, 3], LANES),
                     pool_sk, pool_sv, bs=SEL_BLOCK, nc=1, tks=min(512, past), past=past, from_means=False)
    o_s_list = [_unstack_outputs(o, n_s, s_len, KV_A) for o in (o_c, o_s, o_w)]
    gate_list = [jnp.repeat(gates[:, br * N_HEADS:(br + 1) * N_HEADS], HEAD_DIM, axis=1) for br in range(3)]
    kv6 = kv.reshape(n_s, s_len, 6, KV_A, HEAD_DIM)
    shp = (n_s, -1, KV_A, HEAD_DIM)
    st_s = tuple(kv6[:, :, i] for i in range(4)) + (kw_all[:, s_len:].reshape(shp), vw_all[:, s_len:].reshape(shp))
    return o_p, o_s_list, gate_list, st_p, st_s


def _swa_layer(hp_x, hs_x, g_pre, w_in, sinks, bufs, past, slopes, n_p, t, n_s, s_len):
    kvw = 2 * KV_B * HEAD_DIM
    wq = w_in[:, :Q_WIDTH].astype(BF16)
    wkv = w_in[:, Q_WIDTH:].astype(BF16)
    gw = KV_B * HEAD_DIM

    q, kv, kvb = norm_proj(hp_x, g_pre, wq, wkv)
    o_hm = swa_prompt_attn(slopes, sinks, _head_major(q, n_p, t), _head_major(kvb, n_p, t))
    kv5 = kv.reshape(n_p, t, 2, KV_B, HEAD_DIM)
    wb = min(WIN_B, t)
    st_p = (kv5[:, t - wb:, 0], kv5[:, t - wb:, 1])

    q, kv, kvb = norm_proj(hs_x, g_pre, wq, wkv)
    kv4 = kv.reshape(n_s, s_len, 2, gw)
    buf_k, buf_v = [b.reshape(n_s, b.shape[1], gw) for b in bufs]
    k_all = jnp.concatenate([buf_k, kv4[:, :, 0]], axis=1)
    v_all = jnp.concatenate([buf_v, kv4[:, :, 1]], axis=1)
    o = swa_sample_attn(slopes, sinks, _stack_queries(q, n_s, s_len, KV_B), k_all, v_all, past)
    shp = (n_s, -1, KV_B, HEAD_DIM)
    st_s = (k_all[:, s_len:].reshape(shp), v_all[:, s_len:].reshape(shp))
    return [_token_major(o_hm)], [_unstack_outputs(o, n_s, s_len, KV_B)], st_p, st_s


def _split_hi_lo(x):
    hi = x.astype(BF16)
    return hi, (x - hi.astype(F32)).astype(BF16)


def _moba_layer(hp_x, hs_x, g_pre, w_in, pools, ptab, slopes, n_p, t, n_s, s_len):
    kvw = 2 * KV_C * HEAD_DIM
    wq = w_in[:, :Q_WIDTH].astype(BF16)
    wkv = w_in[:, Q_WIDTH:].astype(BF16)
    gw = KV_C * HEAD_DIM
    past = ptab.shape[1] * PAGE

    q, kv, kvb = norm_proj(hp_x, g_pre, wq, wkv)
    kmean = block_means_prompt(kv.reshape(n_p, t, kvw))
    kmean = _head_major(_pad_rows(kmean, LANES).reshape(n_p * LANES, gw), n_p, LANES)
    kmh, kml = _split_hi_lo(kmean)
    o_hm = moba_prompt_attn(slopes, _head_major(q, n_p, t), _head_major(kvb, n_p, t), kmh, kml)
    kv5 = kv.reshape(n_p, t, 2, KV_C, HEAD_DIM)
    st_p = (kv5[:, :, 0], kv5[:, :, 1])

    q, kv, kvb = norm_proj(hs_x, g_pre, wq, wkv)
    kv4 = kv.reshape(n_s, s_len, 2, gw)
    pool_k, pool_v = [p.reshape(p.shape[0], PAGE, gw) for p in pools]
    nc = 2 if past * gw * 4 > (4 << 20) else 1
    kmean = _pad_rows(block_means_sample(ptab, pool_k, nc), LANES)
    o = paged_attn(ptab, slopes, _stack_queries(q, n_s, s_len, KV_C), kmean,
                   _pad_rows(kv4[:, :, 0], LANES), _pad_rows(kv4[:, :, 1], LANES), pool_k, pool_v,
                   bs=MOBA_BLOCK, nc=nc, tks=MOBA_BLOCK, past=past, from_means=True)
    kv5 = kv.reshape(n_s, s_len, 2, KV_C, HEAD_DIM)
    st_s = (kv5[:, :, 0], kv5[:, :, 1])
    return [_token_major(o_hm)], [_unstack_outputs(o, n_s, s_len, KV_C)], st_p, st_s


def kernel(x_prompt, x_sample, cache_nsa_cmp_k, cache_nsa_cmp_v, cache_nsa_sel_k, cache_nsa_sel_v, cache_nsa_win_k, cache_nsa_win_v, cache_swa_k, cache_swa_v, cache_moba_k, cache_moba_v, page_table, norm_pre, norm_post, ffn_w_gate, ffn_w_up, ffn_w_down, nsa_w_in, nsa_cmp_pos_k, nsa_cmp_w1_k, nsa_cmp_w2_k, nsa_cmp_pos_v, nsa_cmp_w1_v, nsa_cmp_w2_v, nsa_w_out, swa_w_in, swa_sinks, swa_w_out, moba_w_in, moba_w_out):
    n_p, t, d = x_prompt.shape
    n_s, s_len, _ = x_sample.shape
    depth = norm_pre.shape[0]
    slopes = jnp.exp2(-8.0 * jnp.arange(1, N_HEADS + 1, dtype=F32) / N_HEADS)
    past = page_table.shape[1] * PAGE
    xp = x_prompt.reshape(n_p * t, d)
    xs = x_sample.reshape(n_s * s_len, d)
    nsa_p, nsa_s, swa_p, swa_s, moba_p, moba_s = [], [], [], [], [], []

    def ffn_both(xp, xs, i, which, slot):
        wg, wu, wd = (w[i, which].astype(BF16) for w in (ffn_w_gate, ffn_w_up, ffn_w_down))
        return (ffn(xp, norm_pre[i, slot], norm_post[i, slot], wg, wu, wd),
                ffn(xs, norm_pre[i, slot], norm_post[i, slot], wg, wu, wd))

    for i in range(depth):
        kind, j = i % 3, i // 3
        xp, xs = ffn_both(xp, xs, i, 0, 0)
        gates_s = None
        if kind == 0:
            cw = {"k": _compress_weights(nsa_cmp_pos_k[j], nsa_cmp_w1_k[j], nsa_cmp_w2_k[j]),
                  "v": _compress_weights(nsa_cmp_pos_v[j], nsa_cmp_w1_v[j], nsa_cmp_w2_v[j])}
            o_p, o_s, gates_s, st_p, st_s = _nsa_layer(
                xp, xs, norm_pre[i, 1], nsa_w_in[j], cw,
                (cache_nsa_cmp_k[j], cache_nsa_cmp_v[j], cache_nsa_sel_k[j], cache_nsa_sel_v[j]),
                (cache_nsa_win_k[j], cache_nsa_win_v[j]), page_table, slopes, n_p, t, n_s, s_len)
            nsa_p.append(st_p)
            nsa_s.append(st_s)
            w_out = nsa_w_out[j]
        elif kind == 1:
            o_p, o_s, st_p, st_s = _swa_layer(xp, xs, norm_pre[i, 1], swa_w_in[j], swa_sinks[j],
                                              (cache_swa_k[j], cache_swa_v[j]), past, slopes, n_p, t, n_s, s_len)
            swa_p.append(st_p)
            swa_s.append(st_s)
            w_out = swa_w_out[j]
        else:
            o_p, o_s, st_p, st_s = _moba_layer(xp, xs, norm_pre[i, 1], moba_w_in[j],
                                               (cache_moba_k[j], cache_moba_v[j]), page_table, slopes,
                                               n_p, t, n_s, s_len)
            moba_p.append(st_p)
            moba_s.append(st_s)
            w_out = moba_w_out[j]
        w_out = w_out.astype(BF16)
        xp = out_proj(o_p, w_out, norm_post[i, 1], xp)
        xs = out_proj(o_s, w_out, norm_post[i, 1], xs, gates_s)
        xp, xs = ffn_both(xp, xs, i, 1, 2)

    stack = lambda states: [jnp.stack(a) for a in zip(*states)]
    return (xp.reshape(n_p, t, d), xs.reshape(n_s, s_len, d),
            *stack(nsa_p), *stack(swa_p), *stack(moba_p), *stack(nsa_s), *stack(swa_s), *stack(moba_s))
```

```python
import functools

import jax
import jax.numpy as jnp
from jax import lax
from jax.experimental import pallas as pl
from jax.experimental.pallas import tpu as pltpu

F32 = jnp.float32
BF16 = jnp.bfloat16
I32 = jnp.int32

D_MODEL = 1024
N_HEADS = 16
HEAD_DIM = 64
Q_WIDTH = N_HEADS * HEAD_DIM
RMS_EPS = 1e-6
ATTN_SCALE = HEAD_DIM ** -0.5
KV_A, KV_B, KV_C = 2, 2, 4
CMP_LEN, CMP_STRIDE, CMP_HID = 32, 16, 256
SEL_BLOCK, SEL_TOPK = 64, 16
WIN_A, WIN_B = 512, 128
MOBA_BLOCK, MOBA_TOPK = 256, 3
PAGE = 128

LANES = 128
TINY = float(jnp.finfo(jnp.float32).tiny)
M_INIT = -1e30
VMEM_LIMIT = 56 * 1024 * 1024

TQ = 128
FFN_TM, FFN_TF = 1024, 256
PROJ_TM = 512
PAGED_TK = 2048
assert WIN_A % TQ == 0 and WIN_B % TQ == 0 and MOBA_BLOCK % TQ == 0 and PAGED_TK % MOBA_BLOCK == 0


def _cparams(sem):
    return pltpu.CompilerParams(dimension_semantics=sem, vmem_limit_bytes=VMEM_LIMIT)


def _mm_nt(a, b):
    return lax.dot_general(a, b, (((1,), (1,)), ((), ())), preferred_element_type=F32)


def _mm(a, b):
    return jnp.dot(a, b, preferred_element_type=F32)


def _rms(x, g):
    return x * lax.rsqrt(jnp.mean(x * x, axis=-1, keepdims=True) + RMS_EPS) * g


def _ffn_kernel(x_ref, gpre_ref, gpost_ref, wg_ref, wu_ref, wd_ref, o_ref, xn_ref, acc_ref):
    f = pl.program_id(1)

    @pl.when(f == 0)
    def _():
        xn_ref[...] = _rms(x_ref[...], gpre_ref[...]).astype(BF16)
        acc_ref[...] = jnp.zeros_like(acc_ref)

    xn = xn_ref[...]
    g = _mm(xn, wg_ref[...])
    u = _mm(xn, wu_ref[...])
    h = g * jax.nn.sigmoid(g) * u
    acc_ref[...] += _mm(h.astype(BF16), wd_ref[...])

    @pl.when(f == pl.num_programs(1) - 1)
    def _():
        o_ref[...] = x_ref[...] + 0.5 * _rms(acc_ref[...], gpost_ref[...])


def ffn(x, g_pre, g_post, wg, wu, wd):
    m, d = x.shape
    dff = wg.shape[1]
    tm = min(FFN_TM, m)
    return pl.pallas_call(
        _ffn_kernel,
        grid=(m // tm, dff // FFN_TF),
        in_specs=[
            pl.BlockSpec((tm, d), lambda i, f: (i, 0)),
            pl.BlockSpec((1, d), lambda i, f: (0, 0)),
            pl.BlockSpec((1, d), lambda i, f: (0, 0)),
            pl.BlockSpec((d, FFN_TF), lambda i, f: (0, f)),
            pl.BlockSpec((d, FFN_TF), lambda i, f: (0, f)),
            pl.BlockSpec((FFN_TF, d), lambda i, f: (f, 0)),
        ],
        out_specs=pl.BlockSpec((tm, d), lambda i, f: (i, 0)),
        out_shape=jax.ShapeDtypeStruct((m, d), F32),
        scratch_shapes=[pltpu.VMEM((tm, d), BF16), pltpu.VMEM((tm, d), F32)],
        compiler_params=_cparams(("parallel", "arbitrary")),
        name="ffn",
    )(x, g_pre.reshape(1, d), g_post.reshape(1, d), wg, wu, wd)


def _proj_prompt_kernel(x_ref, g_ref, wqt_ref, wkvt_ref, wk_ref, *rest, nsa):
    if nsa:
        wc_ref, wgt_ref, qt_ref, kvt_ref, kvtb_ref, k_ref, c_ref, gt_ref = rest
    else:
        qt_ref, kvt_ref, kvtb_ref, k_ref = rest
    xn = _rms(x_ref[0], g_ref[...]).astype(BF16)
    qt_ref[0] = (_mm_nt(wqt_ref[...], xn) * ATTN_SCALE).astype(BF16)
    kvt = _mm_nt(wkvt_ref[...], xn)
    kvt_ref[0] = kvt
    kvtb_ref[0] = kvt.astype(BF16)
    for c in range(wk_ref.shape[0]):
        k_ref[0, c] = _mm(xn, wk_ref[c]).astype(BF16)
    if nsa:
        c_ref[0] = _mm(xn, wc_ref[...])
        gt_ref[0] = jax.nn.sigmoid(_mm_nt(wgt_ref[...], xn))


def proj_prompt(x, g, w_in, k_cols, nsa):
    n, t, d = x.shape
    kvw = w_in.shape[1] - Q_WIDTH - (3 * N_HEADS if nsa else 0)
    wqt = w_in[:, :Q_WIDTH].T.astype(BF16)
    wkv = w_in[:, Q_WIDTH:Q_WIDTH + kvw]
    wkvt = wkv.T.astype(BF16)
    wk = jnp.stack([wkv[:, c * HEAD_DIM:(c + 1) * HEAD_DIM] for c in k_cols]).astype(BF16)
    ck = len(k_cols)
    fixed2 = lambda i, j: (0, 0)
    fixed3 = lambda i, j: (0, 0, 0)
    in_specs = [pl.BlockSpec((1, TQ, d), lambda i, j: (i, j, 0)), pl.BlockSpec((1, d), fixed2),
                pl.BlockSpec((Q_WIDTH, d), fixed2), pl.BlockSpec((kvw, d), fixed2),
                pl.BlockSpec((ck, d, HEAD_DIM), fixed3)]
    out_specs = [pl.BlockSpec((1, Q_WIDTH, TQ), lambda i, j: (i, 0, j)),
                 pl.BlockSpec((1, kvw, TQ), lambda i, j: (i, 0, j)),
                 pl.BlockSpec((1, kvw, TQ), lambda i, j: (i, 0, j)),
                 pl.BlockSpec((1, ck, TQ, HEAD_DIM), lambda i, j: (i, 0, j, 0))]
    out_shape = [jax.ShapeDtypeStruct((n, Q_WIDTH, t), BF16),
                 jax.ShapeDtypeStruct((n, kvw, t), F32), jax.ShapeDtypeStruct((n, kvw, t), BF16),
                 jax.ShapeDtypeStruct((n, ck, t, HEAD_DIM), BF16)]
    args = [x, g.reshape(1, d), wqt, wkvt, wk]
    if nsa:
        wgt = jnp.pad(w_in[:, Q_WIDTH + kvw:], ((0, 0), (0, LANES - 3 * N_HEADS))).T.astype(BF16)
        in_specs += [pl.BlockSpec((d, 2 * LANES), fixed2), pl.BlockSpec((LANES, d), fixed2)]
        out_specs += [pl.BlockSpec((1, TQ, 2 * LANES), lambda i, j: (i, j, 0)),
                      pl.BlockSpec((1, LANES, TQ), lambda i, j: (i, 0, j))]
        out_shape += [jax.ShapeDtypeStruct((n, t, 2 * LANES), F32), jax.ShapeDtypeStruct((n, LANES, t), F32)]
        args += [wkv[:, :2 * LANES].astype(BF16), wgt]
    return pl.pallas_call(
        functools.partial(_proj_prompt_kernel, nsa=nsa),
        grid=(n, t // TQ), in_specs=in_specs, out_specs=out_specs, out_shape=out_shape,
        compiler_params=_cparams(("parallel", "parallel")), name="proj_prompt",
    )(*args)


def _proj_kernel(x_ref, g_ref, wq_ref, wkv_ref, *rest, has_gates):
    if has_gates:
        wg_ref, q_ref, kv_ref, gate_ref = rest
    else:
        q_ref, kv_ref = rest
    xn = _rms(x_ref[...], g_ref[...]).astype(BF16)
    q_ref[...] = (_mm(xn, wq_ref[...]) * ATTN_SCALE).astype(BF16)
    kv_ref[...] = _mm(xn, wkv_ref[...])
    if has_gates:
        gate_ref[...] = jax.nn.sigmoid(_mm(xn, wg_ref[...]))


def proj_sample(x, g, w_in, nsa):
    m, d = x.shape
    kvw = w_in.shape[1] - Q_WIDTH - (3 * N_HEADS if nsa else 0)
    tm = min(PROJ_TM, m)
    row = lambda i: (i, 0)
    fixed = lambda i: (0, 0)
    in_specs = [pl.BlockSpec((tm, d), row), pl.BlockSpec((1, d), fixed),
                pl.BlockSpec((d, Q_WIDTH), fixed), pl.BlockSpec((d, kvw), fixed)]
    out_specs = [pl.BlockSpec((tm, Q_WIDTH), row), pl.BlockSpec((tm, kvw), row)]
    out_shape = [jax.ShapeDtypeStruct((m, Q_WIDTH), BF16), jax.ShapeDtypeStruct((m, kvw), F32)]
    args = [x, g.reshape(1, d), w_in[:, :Q_WIDTH].astype(BF16), w_in[:, Q_WIDTH:Q_WIDTH + kvw].astype(BF16)]
    if nsa:
        in_specs.append(pl.BlockSpec((d, LANES), fixed))
        out_specs.append(pl.BlockSpec((tm, LANES), row))
        out_shape.append(jax.ShapeDtypeStruct((m, LANES), F32))
        args.append(jnp.pad(w_in[:, Q_WIDTH + kvw:], ((0, 0), (0, LANES - 3 * N_HEADS))).astype(BF16))
    return pl.pallas_call(
        functools.partial(_proj_kernel, has_gates=nsa),
        grid=(m // tm,), in_specs=in_specs, out_specs=out_specs, out_shape=out_shape,
        compiler_params=_cparams(("parallel",)), name="proj_sample",
    )(*args)


def _out_prompt_kernel(ot_ref, w_ref, gp_ref, x_ref, y_ref):
    mix = lax.dot_general(ot_ref[0], w_ref[...], (((0,), (0,)), ((), ())), preferred_element_type=F32)
    y_ref[0] = x_ref[0] + _rms(mix, gp_ref[...])


def out_proj_prompt(o_t, w_out, g_post, x):
    n, t, d = x.shape
    return pl.pallas_call(
        _out_prompt_kernel,
        grid=(n, t // TQ),
        in_specs=[pl.BlockSpec((1, Q_WIDTH, TQ), lambda i, j: (i, 0, j)),
                  pl.BlockSpec((Q_WIDTH, d), lambda i, j: (0, 0)),
                  pl.BlockSpec((1, d), lambda i, j: (0, 0)),
                  pl.BlockSpec((1, TQ, d), lambda i, j: (i, j, 0))],
        out_specs=pl.BlockSpec((1, TQ, d), lambda i, j: (i, j, 0)),
        out_shape=jax.ShapeDtypeStruct((n, t, d), F32),
        compiler_params=_cparams(("parallel", "parallel")), name="out_proj_prompt",
    )(o_t, w_out, g_post.reshape(1, d), x)


def _out_kernel(*refs, n_o, gated):
    o_refs = refs[:n_o]
    pos = n_o
    if gated:
        g_refs = refs[pos:pos + n_o]
        pos += n_o
    w_ref, gp_ref, x_ref, y_ref = refs[pos:pos + 4]
    o = None
    for i in range(n_o):
        t = o_refs[i][...]
        if gated:
            t = g_refs[i][...] * t
        o = t if o is None else o + t
    y_ref[...] = x_ref[...] + _rms(_mm(o.astype(BF16), w_ref[...]), gp_ref[...])


def out_proj(o_list, w_out, g_post, x, gate_list=None):
    m, d = x.shape
    tm = min(PROJ_TM, m)
    n_o = len(o_list)
    gated = gate_list is not None
    row = lambda i: (i, 0)
    fixed = lambda i: (0, 0)
    in_specs = [pl.BlockSpec((tm, Q_WIDTH), row)] * (n_o * (2 if gated else 1))
    in_specs += [pl.BlockSpec((Q_WIDTH, d), fixed), pl.BlockSpec((1, d), fixed), pl.BlockSpec((tm, d), row)]
    args = list(o_list) + (list(gate_list) if gated else []) + [w_out, g_post.reshape(1, d), x]
    return pl.pallas_call(
        functools.partial(_out_kernel, n_o=n_o, gated=gated),
        grid=(m // tm,), in_specs=in_specs, out_specs=pl.BlockSpec((tm, d), row),
        out_shape=jax.ShapeDtypeStruct((m, d), F32),
        compiler_params=_cparams(("parallel",)), name="out_proj",
    )(*args)


def _col(ref, heads, rows):
    return jnp.concatenate([jnp.full((rows, 1), ref[h], F32) for h in heads], axis=0)


def _row(ref, heads, cols):
    return jnp.concatenate([jnp.full((1, cols), ref[h], F32) for h in heads], axis=1)


def _group_queries(qt_ref, g, hg):
    base = g * hg * HEAD_DIM
    return jnp.concatenate([qt_ref[0, base + h * HEAD_DIM:base + (h + 1) * HEAD_DIM, :] for h in range(hg)], axis=1)


def _store_group(ot_ref, g, hg, o):
    base = g * hg * HEAD_DIM
    for h in range(hg):
        ot_ref[0, base + h * HEAD_DIM:base + (h + 1) * HEAD_DIM, :] = o[:, h * TQ:(h + 1) * TQ].astype(BF16)


def _expand_blocks(sel, key0, tk, bs):
    nb = sel.shape[1]
    blk = (key0 + lax.broadcasted_iota(I32, (nb, tk), 1)) // bs
    onehot = (blk == lax.broadcasted_iota(I32, (nb, tk), 0)).astype(BF16)
    return _mm(sel, onehot)


def _expand_blocks_t(sel_t, key0, tk, bs):
    nb = sel_t.shape[0]
    blk = (key0 + lax.broadcasted_iota(I32, (tk, nb), 0)) // bs
    onehot = (blk == lax.broadcasted_iota(I32, (tk, nb), 1)).astype(BF16)
    return _mm(onehot, sel_t)


def _topk_mask(sc, nb, topk):
    jidx = lax.broadcasted_iota(I32, sc.shape, 1)
    rank = jnp.zeros(sc.shape, I32)
    for jp in range(nb):
        col = sc[:, jp:jp + 1]
        ahead = (col > sc) | ((col == sc) & (jidx > jp))
        rank = rank + ahead.astype(I32)
    return rank < topk


def _topk_mask_t(sc, nb, topk):
    jidx = lax.broadcasted_iota(I32, sc.shape, 0)
    rank = jnp.zeros(sc.shape, I32)
    for jp in range(nb):
        row = sc[jp:jp + 1, :]
        ahead = (row > sc) | ((row == sc) & (jidx > jp))
        rank = rank + ahead.astype(I32)
    return rank < topk


def _flash_t(qt, k_at, vt_at, segments, tk, qpos, slope, m0, l0, m_ref, l_ref, acc_ref):
    m_ref[...] = m0
    l_ref[...] = l0
    acc_ref[...] = jnp.zeros_like(acc_ref)
    cols = qt.shape[1]
    key_in_tile = lax.broadcasted_iota(I32, (tk, cols), 0)
    rise = slope * key_in_tile.astype(F32)

    def body(kt, carry, mask_fn):
        s = _mm(k_at(kt), qt) + rise - slope * (qpos - kt * tk).astype(F32)
        if mask_fn is not None:
            s = jnp.where(mask_fn(kt, lambda: (qpos - kt * tk) - key_in_tile), s, -jnp.inf)
        m_old = m_ref[...]
        m_new = jnp.maximum(m_old, jnp.max(s, axis=0, keepdims=True))
        alpha = jnp.exp(m_old - m_new)
        p = jnp.exp(s - m_new)
        l_ref[...] = alpha * l_ref[...] + jnp.sum(p, axis=0, keepdims=True)
        acc_ref[...] = alpha * acc_ref[...] + _mm(vt_at(kt), p.astype(BF16))
        m_ref[...] = m_new
        return carry

    for lo, hi, mask_fn in segments:
        lax.fori_loop(lo, hi, functools.partial(body, mask_fn=mask_fn), 0)
    return acc_ref[...] / jnp.maximum(l_ref[...], TINY)


def _softmax_rows(s, sink=None):
    m = jnp.max(s, axis=1, keepdims=True)
    if sink is not None:
        m = jnp.maximum(m, sink)
    m = jnp.where(m > -jnp.inf, m, 0.0)
    e = jnp.exp(s - m)
    den = jnp.sum(e, axis=1, keepdims=True)
    if sink is not None:
        den = den + jnp.exp(sink - m)
    return e / jnp.maximum(den, TINY)


def _softmax_cols(s):
    m = jnp.max(s, axis=0, keepdims=True)
    m = jnp.where(m > -jnp.inf, m, 0.0)
    e = jnp.exp(s - m)
    return e / jnp.maximum(jnp.sum(e, axis=0, keepdims=True), TINY)


def _compress_rows(load_rows, nch, wbig_ref, pos_ref, w2_ref):
    acc = jnp.zeros((nch, 4 * CMP_HID), F32)
    for l in range(CMP_STRIDE):
        acc = acc + _mm(load_rows(l).astype(BF16), wbig_ref[l])
    first = acc[:, :2 * CMP_HID]
    second = pltpu.roll(acc[:, 2 * CMP_HID:], nch - 1, 0)
    hid = first + second + pos_ref[0:1, :]
    return _mm(jax.nn.gelu(hid).astype(BF16), w2_ref[...])


def _compress_prompt_kernel(k_ref, v_ref, wk_ref, pk_ref, w2k_ref, wv_ref, pv_ref, w2v_ref, ok_ref, ov_ref, *, nch):
    for x_ref, w, p, w2, o_ref in ((k_ref, wk_ref, pk_ref, w2k_ref, ok_ref), (v_ref, wv_ref, pv_ref, w2v_ref, ov_ref)):
        load = lambda l, x_ref=x_ref: x_ref[0, pl.ds(l, nch, stride=CMP_STRIDE), :]
        o_ref[0] = _compress_rows(load, nch, w, p, w2).astype(BF16)


def compress_prompt(rows, cw):
    n, t, _ = rows.shape
    nch = t // CMP_STRIDE
    wspec = [pl.BlockSpec((CMP_STRIDE, LANES, 4 * CMP_HID), lambda i: (0, 0, 0)),
             pl.BlockSpec((8, 2 * CMP_HID), lambda i: (0, 0)),
             pl.BlockSpec((2 * CMP_HID, LANES), lambda i: (0, 0))]
    out = jax.ShapeDtypeStruct((n, nch, LANES), BF16)
    return pl.pallas_call(
        functools.partial(_compress_prompt_kernel, nch=nch),
        grid=(n,),
        in_specs=[pl.BlockSpec((1, t, LANES), lambda i: (i, 0, 0)),
                  pl.BlockSpec((1, t, LANES), lambda i: (i, 0, 1))] + wspec + wspec,
        out_specs=[pl.BlockSpec((1, nch, LANES), lambda i: (i, 0, 0))] * 2,
        out_shape=[out, out],
        compiler_params=_cparams(("parallel",)), name="compress_prompt",
    )(rows, rows, *cw["k"], *cw["v"])


def _pos_term_kernel(p_ref, w_ref, o_ref):
    o_ref[...] = jnp.dot(p_ref[...], w_ref[...], precision=lax.Precision.HIGHEST, preferred_element_type=F32)


def _compress_weights(pos, w1, w2):
    eye = jnp.eye(KV_A, dtype=F32)
    w1r = w1.reshape(CMP_LEN // CMP_STRIDE, CMP_STRIDE, HEAD_DIM, CMP_HID)
    wbig = jnp.einsum("jldh,ab->lbdjah", w1r, eye).reshape(CMP_STRIDE, LANES, 4 * CMP_HID).astype(BF16)
    w2big = jnp.einsum("hd,ab->ahbd", w2, eye).reshape(KV_A * CMP_HID, LANES).astype(BF16)
    pos8 = jnp.broadcast_to(pos.reshape(1, CMP_LEN * HEAD_DIM), (8, CMP_LEN * HEAD_DIM))
    pterm = pl.pallas_call(
        _pos_term_kernel, out_shape=jax.ShapeDtypeStruct((8, CMP_HID), F32), name="cmp_pos_term",
    )(pos8, w1)
    return wbig, jnp.concatenate([pterm, pterm], axis=1), w2big


def _page_copies(ptab_ref, pools, bufs, sem, layer, n, c, slot, pps, feature_major):
    copies = []
    for i in range(pps):
        page = ptab_ref[n, c * pps + i]
        for j, (pool, buf) in enumerate(zip(pools, bufs)):
            dst = buf.at[slot, :, pl.ds(i * PAGE, PAGE)] if feature_major else buf.at[slot, pl.ds(i * PAGE, PAGE)]
            copies.append(pltpu.make_async_copy(pool.at[layer, page], dst, sem.at[j, slot]))
    return copies


def _paged_pipeline(ptab_ref, pools, bufs, sem, layer, nc, pps, feature_major):
    n, c = pl.program_id(0), pl.program_id(1)
    step = n * nc + c
    slot = step % 2
    copies = functools.partial(_page_copies, ptab_ref, pools, bufs, sem, layer, pps=pps, feature_major=feature_major)

    @pl.when(step == 0)
    def _():
        for cp in copies(n, c, slot):
            cp.start()

    @pl.when(step + 1 < pl.num_programs(0) * nc)
    def _():
        nxt = step + 1
        for cp in copies(nxt // nc, nxt % nc, 1 - slot):
            cp.start()

    for cp in copies(n, c, slot):
        cp.wait()
    return slot


def _compress_sample_kernel(ptab_ref, kpool, vpool, wk_ref, pk_ref, w2k_ref, wv_ref, pv_ref, w2v_ref,
                            ok_ref, ov_ref, kbuf, vbuf, sem, *, layer, n_pages):
    slot = _paged_pipeline(ptab_ref, (kpool, vpool), (kbuf, vbuf), sem, layer, 1, n_pages, False)
    nch = n_pages * PAGE // CMP_STRIDE
    for buf, w, p, w2, o_ref in ((kbuf, wk_ref, pk_ref, w2k_ref, ok_ref), (vbuf, wv_ref, pv_ref, w2v_ref, ov_ref)):
        load = lambda l, buf=buf: buf[slot, pl.ds(l, nch, stride=CMP_STRIDE), :]
        o_ref[0] = _compress_rows(load, nch, w, p, w2).astype(BF16)


def compress_sample(ptab, pool_k, pool_v, layer, cw):
    n, n_pages = ptab.shape
    rows = n_pages * PAGE
    nch = rows // CMP_STRIDE
    wspec = [pl.BlockSpec((CMP_STRIDE, LANES, 4 * CMP_HID), lambda i, c, pt: (0, 0, 0)),
             pl.BlockSpec((8, 2 * CMP_HID), lambda i, c, pt: (0, 0)),
             pl.BlockSpec((2 * CMP_HID, LANES), lambda i, c, pt: (0, 0))]
    anyspec = pl.BlockSpec(memory_space=pl.ANY)
    out = jax.ShapeDtypeStruct((n, nch, LANES), BF16)
    return pl.pallas_call(
        functools.partial(_compress_sample_kernel, layer=layer, n_pages=n_pages),
        grid_spec=pltpu.PrefetchScalarGridSpec(
            num_scalar_prefetch=1, grid=(n, 1),
            in_specs=[anyspec, anyspec] + wspec + wspec,
            out_specs=[pl.BlockSpec((1, nch, LANES), lambda i, c, pt: (i, 0, 0))] * 2,
            scratch_shapes=[pltpu.VMEM((2, rows, LANES), F32), pltpu.VMEM((2, rows, LANES), F32),
                            pltpu.SemaphoreType.DMA((2, 2))]),
        out_shape=[out, out],
        compiler_params=_cparams(("arbitrary", "arbitrary")), name="compress_sample",
    )(ptab, pool_k, pool_v, *cw["k"], *cw["v"])


def _nsa_prompt_kernel(slopes_ref, qt_ref, k_ref, kvt_ref, kc_ref, vct_ref, gt_ref, ot_ref, m_ref, l_ref, acc_ref,
                       *, t_len):
    qi = pl.program_id(1)
    q0 = qi * TQ
    hg = N_HEADS // KV_A
    cols = hg * TQ
    qpos = q0 + (lax.broadcasted_iota(I32, (1, cols), 1) & (TQ - 1))
    qpos_t = q0 + lax.broadcasted_iota(I32, (1, TQ), 1)
    ncp = kc_ref.shape[2]
    n_cmp = t_len // CMP_STRIDE - CMP_LEN // CMP_STRIDE + 1
    n_sel = t_len // SEL_BLOCK
    m0 = jnp.full((1, cols), M_INIT, F32)
    l0 = jnp.zeros((1, cols), F32)
    gw = KV_A * HEAD_DIM

    for g in range(KV_A):
        heads = range(g * hg, (g + 1) * hg)
        qt = _group_queries(qt_ref, g, hg)
        slope = _row(slopes_ref, heads, TQ)

        cidx = lax.broadcasted_iota(I32, (ncp, cols), 0)
        dist = qpos - (cidx * CMP_STRIDE + CMP_LEN - 1)
        s = _mm(kc_ref[0, g], qt)
        s = jnp.where((dist >= 0) & (cidx < n_cmp), s - slope * dist.astype(F32), -jnp.inf)
        p = _softmax_cols(s)
        o_c = _mm(vct_ref[0, g], p.astype(BF16))
        imp = p[:, 0:TQ]
        for h in range(1, hg):
            imp = imp + p[:, h * TQ:(h + 1) * TQ]

        srow = lax.broadcasted_iota(I32, (LANES, ncp), 0) * SEL_BLOCK
        ccol = lax.broadcasted_iota(I32, (LANES, ncp), 1) * CMP_STRIDE
        overlap = ((ccol < srow + SEL_BLOCK) & (ccol + CMP_LEN > srow)).astype(F32)
        score = jnp.dot(overlap, imp, precision=lax.Precision.HIGHEST, preferred_element_type=F32)
        jidx = lax.broadcasted_iota(I32, (LANES, TQ), 0)
        qb = qpos_t // SEL_BLOCK
        forced = (jidx == 0) | (jidx == qb) | (jidx == qb - 1)
        causal = jidx <= qb
        sc = jnp.where(forced, jnp.inf, jnp.where(causal, score, -jnp.inf))
        sel = (_topk_mask_t(sc, n_sel, SEL_TOPK) & causal).astype(BF16)

        def picked(kt, dist, sel=sel):
            e = _expand_blocks_t(sel, kt * TQ, TQ, SEL_BLOCK)
            return jnp.concatenate([e] * hg, axis=1) > 0.5

        tile = lambda kt: pl.ds(pl.multiple_of(kt * TQ, TQ), TQ)
        o_s = _flash_t(qt, lambda kt, g=g: k_ref[0, g, tile(kt), :],
                       lambda kt, g=g: kvt_ref[0, 3 * gw + g * HEAD_DIM:3 * gw + (g + 1) * HEAD_DIM, tile(kt)],
                       [(0, qi, picked), (qi, qi + 1, lambda kt, dist: picked(kt, dist) & (dist() >= 0))],
                       TQ, qpos, slope, m0, l0, m_ref, l_ref, acc_ref)

        back = -(-(WIN_A - 1) // TQ)
        o_w = _flash_t(qt, lambda kt, g=g: k_ref[0, KV_A + g, tile(kt), :],
                       lambda kt, g=g: kvt_ref[0, 5 * gw + g * HEAD_DIM:5 * gw + (g + 1) * HEAD_DIM, tile(kt)],
                       [(jnp.maximum(qi - back, 0), jnp.maximum(qi - back + 1, 0), lambda kt, dist: dist() < WIN_A),
                        (jnp.maximum(qi - back + 1, 0), qi, None),
                        (qi, qi + 1, lambda kt, dist: dist() >= 0)],
                       TQ, qpos, slope, m0, l0, m_ref, l_ref, acc_ref)

        gates = gt_ref[0]
        grow = lambda br: jnp.concatenate([gates[br * N_HEADS + h:br * N_HEADS + h + 1, :] for h in heads], axis=1)
        _store_group(ot_ref, g, hg, grow(0) * o_c + grow(1) * o_s + grow(2) * o_w)


def nsa_prompt_attn(slopes, q_t, k_rows, kv_t, kc, vc_t, gates_t):
    n, _, t = q_t.shape
    hg = N_HEADS // KV_A
    ncp = kc.shape[2]
    cols = hg * TQ
    return pl.pallas_call(
        functools.partial(_nsa_prompt_kernel, t_len=t),
        grid_spec=pltpu.PrefetchScalarGridSpec(
            num_scalar_prefetch=1, grid=(n, t // TQ),
            in_specs=[pl.BlockSpec((1, Q_WIDTH, TQ), lambda i, j, s: (i, 0, j)),
                      pl.BlockSpec((1, 2 * KV_A, t, HEAD_DIM), lambda i, j, s: (i, 0, 0, 0)),
                      pl.BlockSpec((1, kv_t.shape[1], t), lambda i, j, s: (i, 0, 0)),
                      pl.BlockSpec((1, KV_A, ncp, HEAD_DIM), lambda i, j, s: (i, 0, 0, 0)),
                      pl.BlockSpec((1, KV_A, HEAD_DIM, ncp), lambda i, j, s: (i, 0, 0, 0)),
                      pl.BlockSpec((1, LANES, TQ), lambda i, j, s: (i, 0, j))],
            out_specs=pl.BlockSpec((1, Q_WIDTH, TQ), lambda i, j, s: (i, 0, j)),
            scratch_shapes=[pltpu.VMEM((1, cols), F32), pltpu.VMEM((1, cols), F32),
                            pltpu.VMEM((HEAD_DIM, cols), F32)]),
        out_shape=jax.ShapeDtypeStruct((n, Q_WIDTH, t), BF16),
        compiler_params=_cparams(("parallel", "arbitrary")), name="nsa_prompt_attn",
    )(slopes, q_t, k_rows, kv_t, kc, vc_t, gates_t)


def _swa_prompt_kernel(slopes_ref, sinks_ref, qt_ref, k_ref, kvt_ref, ot_ref, m_ref, l_ref, acc_ref):
    qi = pl.program_id(1)
    hg = N_HEADS // KV_B
    cols = hg * TQ
    gw = KV_B * HEAD_DIM
    qpos = qi * TQ + (lax.broadcasted_iota(I32, (1, cols), 1) & (TQ - 1))
    back = -(-(WIN_B - 1) // TQ)
    segments = [(jnp.maximum(qi - back, 0), qi, lambda kt, dist: dist() < WIN_B),
                (qi, qi + 1, lambda kt, dist: dist() >= 0)]
    tile = lambda kt: pl.ds(pl.multiple_of(kt * TQ, TQ), TQ)
    for g in range(KV_B):
        heads = range(g * hg, (g + 1) * hg)
        o = _flash_t(_group_queries(qt_ref, g, hg), lambda kt, g=g: k_ref[0, g, tile(kt), :],
                     lambda kt, g=g: kvt_ref[0, gw + g * HEAD_DIM:gw + (g + 1) * HEAD_DIM, tile(kt)],
                     segments, TQ, qpos, _row(slopes_ref, heads, TQ),
                     _row(sinks_ref, heads, TQ), jnp.ones((1, cols), F32), m_ref, l_ref, acc_ref)
        _store_group(ot_ref, g, hg, o)


def swa_prompt_attn(slopes, sinks, q_t, k_rows, kv_t):
    n, _, t = q_t.shape
    hg = N_HEADS // KV_B
    cols = hg * TQ
    return pl.pallas_call(
        _swa_prompt_kernel,
        grid_spec=pltpu.PrefetchScalarGridSpec(
            num_scalar_prefetch=2, grid=(n, t // TQ),
            in_specs=[pl.BlockSpec((1, Q_WIDTH, TQ), lambda i, j, s, k: (i, 0, j)),
                      pl.BlockSpec((1, KV_B, t, HEAD_DIM), lambda i, j, s, k: (i, 0, 0, 0)),
                      pl.BlockSpec((1, kv_t.shape[1], t), lambda i, j, s, k: (i, 0, 0))],
            out_specs=pl.BlockSpec((1, Q_WIDTH, TQ), lambda i, j, s, k: (i, 0, j)),
            scratch_shapes=[pltpu.VMEM((1, cols), F32), pltpu.VMEM((1, cols), F32),
                            pltpu.VMEM((HEAD_DIM, cols), F32)]),
        out_shape=jax.ShapeDtypeStruct((n, Q_WIDTH, t), BF16),
        compiler_params=_cparams(("parallel", "arbitrary")), name="swa_prompt_attn",
    )(slopes, sinks, q_t, k_rows, kv_t)


def _block_mean_prompt_kernel(kt_ref, o_ref, *, nb):
    lane = lax.broadcasted_iota(I32, o_ref.shape[1:], 1)
    out = jnp.zeros(o_ref.shape[1:], F32)
    for b in range(nb):
        blk = kt_ref[0, :, b * MOBA_BLOCK:(b + 1) * MOBA_BLOCK]
        out = jnp.where(lane == b, jnp.sum(blk, axis=1, keepdims=True) / MOBA_BLOCK, out)
    o_ref[0] = out


def block_means_prompt(kv_t):
    n, _, t = kv_t.shape
    w = KV_C * HEAD_DIM
    return pl.pallas_call(
        functools.partial(_block_mean_prompt_kernel, nb=t // MOBA_BLOCK),
        grid=(n,),
        in_specs=[pl.BlockSpec((1, w, t), lambda i: (i, 0, 0))],
        out_specs=pl.BlockSpec((1, w, LANES), lambda i: (i, 0, 0)),
        out_shape=jax.ShapeDtypeStruct((n, w, LANES), F32),
        compiler_params=_cparams(("parallel",)), name="block_means_prompt",
    )(kv_t)


def _moba_prompt_kernel(slopes_ref, qt_ref, k_ref, kvt_ref, kmh_ref, kml_ref, ot_ref, m_ref, l_ref, acc_ref, sel_ref,
                        *, t_len):
    qi = pl.program_id(1)
    q0 = qi * TQ
    hg = N_HEADS // KV_C
    cols = hg * TQ
    tk = MOBA_BLOCK
    gw = KV_C * HEAD_DIM
    nb = t_len // MOBA_BLOCK
    qpos = q0 + (lax.broadcasted_iota(I32, (1, cols), 1) & (TQ - 1))
    m0 = jnp.full((1, cols), M_INIT, F32)
    l0 = jnp.zeros((1, cols), F32)
    bidx = lax.broadcasted_iota(I32, (LANES, cols), 0)
    qb = qpos // MOBA_BLOCK
    tile = lambda kt: pl.ds(pl.multiple_of(kt * tk, tk), tk)
    for g in range(KV_C):
        heads = range(g * hg, (g + 1) * hg)
        qt = _group_queries(qt_ref, g, hg)
        gate = _mm(kmh_ref[0, g], qt) + _mm(kml_ref[0, g], qt)
        past_blk = bidx < qb
        sc = jnp.where(past_blk, gate, -jnp.inf)
        sel_ref[...] = ((_topk_mask_t(sc, nb, MOBA_TOPK) & past_blk) | (bidx == qb)).astype(F32)

        picked = lambda kt, dist: sel_ref[pl.ds(kt, 1), :] > 0.5
        last = (q0 + TQ - 1) // tk
        o = _flash_t(qt, lambda kt, g=g: k_ref[0, g, tile(kt), :],
                     lambda kt, g=g: kvt_ref[0, gw + g * HEAD_DIM:gw + (g + 1) * HEAD_DIM, tile(kt)],
                     [(0, last, picked), (last, last + 1, lambda kt, dist: dist() >= 0)],
                     tk, qpos, _row(slopes_ref, heads, TQ), m0, l0, m_ref, l_ref, acc_ref)
        _store_group(ot_ref, g, hg, o)


def moba_prompt_attn(slopes, q_t, k_rows, kv_t, kmh, kml):
    n, _, t = q_t.shape
    hg = N_HEADS // KV_C
    cols = hg * TQ
    return pl.pallas_call(
        functools.partial(_moba_prompt_kernel, t_len=t),
        grid_spec=pltpu.PrefetchScalarGridSpec(
            num_scalar_prefetch=1, grid=(n, t // TQ),
            in_specs=[pl.BlockSpec((1, Q_WIDTH, TQ), lambda i, j, s: (i, 0, j)),
                      pl.BlockSpec((1, KV_C, t, HEAD_DIM), lambda i, j, s: (i, 0, 0, 0)),
                      pl.BlockSpec((1, kv_t.shape[1], t), lambda i, j, s: (i, 0, 0)),
                      pl.BlockSpec((1, KV_C, LANES, HEAD_DIM), lambda i, j, s: (i, 0, 0, 0)),
                      pl.BlockSpec((1, KV_C, LANES, HEAD_DIM), lambda i, j, s: (i, 0, 0, 0))],
            out_specs=pl.BlockSpec((1, Q_WIDTH, TQ), lambda i, j, s: (i, 0, j)),
            scratch_shapes=[pltpu.VMEM((1, cols), F32), pltpu.VMEM((1, cols), F32),
                            pltpu.VMEM((HEAD_DIM, cols), F32), pltpu.VMEM((LANES, cols), F32)]),
        out_shape=jax.ShapeDtypeStruct((n, Q_WIDTH, t), BF16),
        compiler_params=_cparams(("parallel", "arbitrary")), name="moba_prompt_attn",
    )(slopes, q_t, k_rows, kv_t, kmh, kml)


def _sample_cols(slopes_ref, s_len, past):
    slope = _col(slopes_ref, range(N_HEADS), s_len)
    qpos = past + (lax.broadcasted_iota(I32, (N_HEADS * s_len, 1), 0) % s_len)
    return slope, qpos


def _window_sample(qx, kt_all, vt_all, qpos, slope, wb, window, past, sink=None):
    kpos = past - wb + lax.broadcasted_iota(I32, (1, kt_all.shape[1]), 1)
    dist = qpos - kpos
    s = _mm(qx, kt_all.astype(BF16))
    s = jnp.where((dist >= 0) & (dist < window) & (kpos >= 0), s - slope * dist.astype(F32), -jnp.inf)
    p = _softmax_rows(s, sink)
    return _mm_nt(p.astype(BF16), vt_all.astype(BF16))


def _nsa_sample_small_kernel(slopes_ref, qx_ref, kc_ref, vc_ref, kw_ref, vw_ref, oc_ref, ow_ref, sel_ref,
                             *, s_len, past, wb):
    hg = N_HEADS // KV_A
    slope, qpos = _sample_cols(slopes_ref, s_len, past)
    qx = qx_ref[0]
    ncp = kc_ref.shape[1]
    n_cmp = past // CMP_STRIDE - CMP_LEN // CMP_STRIDE + 1
    n_sel = -(-(past + s_len) // SEL_BLOCK)
    nbl = sel_ref.shape[2]

    cidx = lax.broadcasted_iota(I32, (1, ncp), 1)
    dist = qpos - (cidx * CMP_STRIDE + CMP_LEN - 1)
    s = _mm_nt(qx, kc_ref[0])
    s = jnp.where((dist >= 0) & (cidx < n_cmp), s - slope * dist.astype(F32), -jnp.inf)
    p = _softmax_rows(s)
    oc_ref[0] = _mm(p.astype(BF16), vc_ref[0])

    imps = []
    for g in range(KV_A):
        base = g * hg * s_len
        imp = p[base:base + s_len]
        for h in range(1, hg):
            imp = imp + p[base + h * s_len:base + (h + 1) * s_len]
        imps.append(imp)
    imp = jnp.concatenate(imps, axis=0)
    crow = lax.broadcasted_iota(I32, (ncp, nbl), 0) * CMP_STRIDE
    scol = lax.broadcasted_iota(I32, (ncp, nbl), 1) * SEL_BLOCK
    overlap = ((crow < scol + SEL_BLOCK) & (crow + CMP_LEN > scol)
               & (crow < n_cmp * CMP_STRIDE)).astype(F32)
    score = jnp.dot(imp, overlap, precision=lax.Precision.HIGHEST, preferred_element_type=F32)
    jidx = lax.broadcasted_iota(I32, (1, nbl), 1)
    tpos = past + (lax.broadcasted_iota(I32, (KV_A * s_len, 1), 0) % s_len)
    qb = tpos // SEL_BLOCK
    forced = (jidx == 0) | (jidx == qb) | (jidx == qb - 1)
    causal = jidx <= qb
    sc = jnp.where(forced, jnp.inf, jnp.where(causal, score, -jnp.inf))
    sel = (_topk_mask(sc, n_sel, SEL_TOPK) & causal).astype(F32)
    sel_ref[0] = jnp.concatenate([sel[g * s_len:(g + 1) * s_len] for g in range(KV_A) for _ in range(hg)], axis=0)

    ow_ref[0] = _window_sample(qx, kw_ref[0], vw_ref[0], qpos, slope, wb, WIN_A, past)


def nsa_sample_small(slopes, qx, kcmp, vcmp, kwt_all, vwt_all, past, wb, nbl):
    n, rows, w = qx.shape
    s_len = rows // N_HEADS
    ncp = kcmp.shape[1]
    lw = kwt_all.shape[2]
    per_seq = lambda shape: pl.BlockSpec((1,) + shape, lambda i, s: (i, 0, 0))
    return pl.pallas_call(
        functools.partial(_nsa_sample_small_kernel, s_len=s_len, past=past, wb=wb),
        grid_spec=pltpu.PrefetchScalarGridSpec(
            num_scalar_prefetch=1, grid=(n,),
            in_specs=[per_seq((rows, w)), per_seq((ncp, w)), per_seq((ncp, w)), per_seq((w, lw)), per_seq((w, lw))],
            out_specs=[per_seq((rows, w)), per_seq((rows, w)), per_seq((rows, nbl))]),
        out_shape=[jax.ShapeDtypeStruct((n, rows, w), F32), jax.ShapeDtypeStruct((n, rows, w), F32),
                   jax.ShapeDtypeStruct((n, rows, nbl), F32)],
        compiler_params=_cparams(("parallel",)), name="nsa_sample_small",
    )(slopes, qx, kcmp, vcmp, kwt_all, vwt_all)


def _swa_sample_kernel(slopes_ref, sinks_ref, qx_ref, k_ref, v_ref, o_ref, *, s_len, past, wb):
    slope, qpos = _sample_cols(slopes_ref, s_len, past)
    sink = _col(sinks_ref, range(N_HEADS), s_len)
    o_ref[0] = _window_sample(qx_ref[0], k_ref[0], v_ref[0], qpos, slope, wb, WIN_B, past, sink)


def swa_sample_attn(slopes, sinks, qx, kt_all, vt_all, past, wb):
    n, rows, w = qx.shape
    s_len = rows // N_HEADS
    lw = kt_all.shape[2]
    per_seq = lambda shape: pl.BlockSpec((1,) + shape, lambda i, s, k: (i, 0, 0))
    return pl.pallas_call(
        functools.partial(_swa_sample_kernel, s_len=s_len, past=past, wb=wb),
        grid_spec=pltpu.PrefetchScalarGridSpec(
            num_scalar_prefetch=2, grid=(n,),
            in_specs=[per_seq((rows, w)), per_seq((w, lw)), per_seq((w, lw))],
            out_specs=per_seq((rows, w))),
        out_shape=jax.ShapeDtypeStruct((n, rows, w), F32),
        compiler_params=_cparams(("parallel",)), name="swa_sample_attn",
    )(slopes, sinks, qx, kt_all, vt_all)


def _paged_attn_kernel(ptab_ref, slopes_ref, qx_ref, sel_ref, knew_ref, vnew_ref, kpool, vpool, o_ref,
                       kbuf, vbuf, sem, m_ref, l_ref, acc_ref, selb_ref,
                       *, layer, n_pages, bs, tks, s_len, past, from_means):
    slot = _paged_pipeline(ptab_ref, (kpool, vpool), (kbuf, vbuf), sem, layer, 1, n_pages, True)
    slope, qpos = _sample_cols(slopes_ref, s_len, past)
    qx = qx_ref[0]
    nbw = selb_ref.shape[1] - LANES

    m_ref[...] = jnp.full_like(m_ref, M_INIT)
    l_ref[...] = jnp.zeros_like(l_ref)
    acc_ref[...] = jnp.zeros_like(acc_ref)
    selb_ref[:, nbw:] = jnp.zeros((selb_ref.shape[0], LANES), F32)
    if from_means:
        km = sel_ref[0]
        kmh = km.astype(BF16)
        kml = (km - kmh.astype(F32)).astype(BF16)
        gate = _mm(qx, kmh) + _mm(qx, kml)
        bidx = lax.broadcasted_iota(I32, (1, km.shape[1]), 1)
        past_blk = bidx < qpos // bs
        sc = jnp.where(past_blk, gate, -jnp.inf)
        selb_ref[:, :nbw] = (_topk_mask(sc, past // bs, MOBA_TOPK) & past_blk).astype(F32)
    else:
        selb_ref[:, :nbw] = sel_ref[0]

    def update(s, vt):
        m_old = m_ref[...]
        m_new = jnp.maximum(m_old, jnp.max(s, axis=1, keepdims=True))
        alpha = jnp.exp(m_old - m_new)
        p = jnp.exp(s - m_new)
        l_ref[...] = alpha * l_ref[...] + jnp.sum(p, axis=1, keepdims=True)
        acc_ref[...] = alpha * acc_ref[...] + _mm_nt(p.astype(BF16), vt)
        m_ref[...] = m_new

    key_in_tile = lax.broadcasted_iota(I32, (1, tks), 1)
    onehot = (lax.broadcasted_iota(I32, (LANES, tks), 1) // bs
              == lax.broadcasted_iota(I32, (LANES, tks), 0)).astype(BF16)
    rise = slope * key_in_tile.astype(F32)
    for t in range(n_pages * PAGE // tks):
        kt = kbuf[slot, :, t * tks:(t + 1) * tks].astype(BF16)
        vt = vbuf[slot, :, t * tks:(t + 1) * tks].astype(BF16)
        blk0 = t * (tks // bs)
        keep = _mm(selb_ref[:, blk0:blk0 + LANES].astype(BF16), onehot) > 0.5
        s = _mm(qx, kt) + rise - slope * (qpos - t * tks).astype(F32)
        update(jnp.where(keep, s, -jnp.inf), vt)

    j = lax.broadcasted_iota(I32, (1, knew_ref.shape[2]), 1)
    dist = qpos - (past + j)
    s = _mm(qx, knew_ref[0].astype(BF16))
    s = jnp.where((dist >= 0) & (j < s_len), s - slope * dist.astype(F32), -jnp.inf)
    update(s, vnew_ref[0].astype(BF16))
    o_ref[0] = acc_ref[...] / jnp.maximum(l_ref[...], TINY)


def paged_attn(ptab, slopes, qx, sel, knew_t, vnew_t, pool_k, pool_v, layer, *, bs, tks, past, from_means):
    n, n_pages = ptab.shape
    _, rows, w = qx.shape
    s_len = rows // N_HEADS
    nbw = sel.shape[2]
    per_seq = lambda shape: pl.BlockSpec((1,) + shape, lambda i, c, pt, sl: (i, 0, 0))
    anyspec = pl.BlockSpec(memory_space=pl.ANY)
    return pl.pallas_call(
        functools.partial(_paged_attn_kernel, layer=layer, n_pages=n_pages, bs=bs, tks=tks, s_len=s_len, past=past,
                          from_means=from_means),
        grid_spec=pltpu.PrefetchScalarGridSpec(
            num_scalar_prefetch=2, grid=(n, 1),
            in_specs=[per_seq((rows, w)), per_seq(sel.shape[1:]), per_seq(knew_t.shape[1:]),
                      per_seq(vnew_t.shape[1:]), anyspec, anyspec],
            out_specs=per_seq((rows, w)),
            scratch_shapes=[pltpu.VMEM((2, w, n_pages * PAGE), F32), pltpu.VMEM((2, w, n_pages * PAGE), F32),
                            pltpu.SemaphoreType.DMA((2, 2)),
                            pltpu.VMEM((rows, 1), F32), pltpu.VMEM((rows, 1), F32), pltpu.VMEM((rows, w), F32),
                            pltpu.VMEM((rows, nbw + LANES), F32)]),
        out_shape=jax.ShapeDtypeStruct((n, rows, w), F32),
        compiler_params=_cparams(("arbitrary", "arbitrary")), name="paged_attn",
    )(ptab, slopes, qx, sel, knew_t, vnew_t, pool_k, pool_v)


def _block_mean_sample_kernel(ptab_ref, kpool, o_ref, kbuf, sem, *, layer, nc, pps):
    c = pl.program_id(1)
    slot = _paged_pipeline(ptab_ref, (kpool,), (kbuf,), sem, layer, nc, pps, True)
    nbc = pps * PAGE // MOBA_BLOCK
    lane = lax.broadcasted_iota(I32, o_ref.shape[1:], 1)

    @pl.when(c == 0)
    def _():
        o_ref[0] = jnp.zeros(o_ref.shape[1:], F32)

    out = o_ref[0]
    for b in range(nbc):
        blk = kbuf[slot, :, b * MOBA_BLOCK:(b + 1) * MOBA_BLOCK]
        out = jnp.where(lane == c * nbc + b, jnp.sum(blk, axis=1, keepdims=True) / MOBA_BLOCK, out)
    o_ref[0] = out


def block_means_sample(ptab, pool_k, layer, nc):
    n, n_pages = ptab.shape
    w = pool_k.shape[2]
    pps = n_pages // nc
    return pl.pallas_call(
        functools.partial(_block_mean_sample_kernel, layer=layer, nc=nc, pps=pps),
        grid_spec=pltpu.PrefetchScalarGridSpec(
            num_scalar_prefetch=1, grid=(n, nc),
            in_specs=[pl.BlockSpec(memory_space=pl.ANY)],
            out_specs=pl.BlockSpec((1, w, LANES), lambda i, c, pt: (i, 0, 0)),
            scratch_shapes=[pltpu.VMEM((2, w, pps * PAGE), F32), pltpu.SemaphoreType.DMA((1, 2))]),
        out_shape=jax.ShapeDtypeStruct((n, w, LANES), F32),
        compiler_params=_cparams(("arbitrary", "arbitrary")), name="block_means_sample",
    )(ptab, pool_k)


def _stack_queries(q, n, s_len, groups):
    hg = N_HEADS // groups
    qh = q.reshape(n, s_len, groups, hg, 1, HEAD_DIM).transpose(0, 2, 3, 1, 4, 5)
    eye = jnp.eye(groups, dtype=q.dtype).reshape(1, groups, 1, 1, groups, 1)
    return (qh * eye).reshape(n, N_HEADS * s_len, groups * HEAD_DIM)


def _unstack_outputs(o, n, s_len, groups):
    hg = N_HEADS // groups
    o6 = o.reshape(n, groups, hg, s_len, groups, HEAD_DIM)
    diag = jnp.stack([o6[:, g, :, :, g, :] for g in range(groups)], axis=1)
    return diag.transpose(0, 3, 1, 2, 4).reshape(n * s_len, Q_WIDTH)


def _feature_major(cache):
    l, p, r, g, d = cache.shape
    return cache.transpose(0, 1, 3, 4, 2).reshape(l, p, g * d, r)


def _state(kv_t, branch, groups):
    n, _, t = kv_t.shape
    gw = groups * HEAD_DIM
    return kv_t[:, branch * gw:(branch + 1) * gw].reshape(n, groups, HEAD_DIM, t).transpose(0, 3, 1, 2)


def _window_with_new(buf_t, new_rows, groups):
    n, w, wb = buf_t.shape
    s_len = new_rows.shape[1]
    both = jnp.concatenate([buf_t, new_rows.transpose(0, 2, 1)], axis=2)
    state = both[:, :, s_len:].reshape(n, groups, HEAD_DIM, wb).transpose(0, 3, 1, 2)
    lw = -(-(wb + s_len) // LANES) * LANES
    return jnp.pad(both, ((0, 0), (0, 0), (0, lw - wb - s_len))), state


def _new_rows_t(rows):
    return jnp.pad(rows.transpose(0, 2, 1), ((0, 0), (0, 0), (0, LANES - rows.shape[1])))


def _nsa_layer(xp, xs, g_pre, w_in, cw, layer, caches, ptab, slopes):
    cmp_k, cmp_v, sel_k, sel_v, win_k, win_v = caches
    n_p, t, _ = xp.shape
    n_s = ptab.shape[0]
    s_len = xs.shape[0] // n_s
    past = ptab.shape[1] * PAGE
    gw = KV_A * HEAD_DIM

    q_t, kv_t, kv_tb, k_rows, cmp_rows, gates_t = proj_prompt(xp, g_pre, w_in, (4, 5, 8, 9), True)
    kcmp, vcmp = compress_prompt(cmp_rows, cw)
    ncp = max(kcmp.shape[1], LANES)
    pad = ((0, 0), (0, ncp - kcmp.shape[1]), (0, 0))
    kc = jnp.pad(kcmp, pad).reshape(n_p, ncp, KV_A, HEAD_DIM).transpose(0, 2, 1, 3)
    vc_t = jnp.pad(vcmp, pad).reshape(n_p, ncp, KV_A, HEAD_DIM).transpose(0, 2, 3, 1)
    o_t = nsa_prompt_attn(slopes, q_t, k_rows, kv_tb, kc, vc_t, gates_t)
    wb = min(WIN_A, t)
    st_p = tuple(_state(kv_t, i, KV_A) for i in range(4)) + tuple(_state(kv_t[:, :, t - wb:], i, KV_A) for i in (4, 5))

    q, kv, gates = proj_sample(xs, g_pre, w_in, True)
    kv4 = kv.reshape(n_s, s_len, 6, gw)
    kwt_all, st_wk = _window_with_new(_feature_major(win_k)[layer], kv4[:, :, 4], KV_A)
    vwt_all, st_wv = _window_with_new(_feature_major(win_v)[layer], kv4[:, :, 5], KV_A)
    row_pages = lambda c: c.reshape(c.shape[0], c.shape[1], PAGE, gw)
    kcmp, vcmp = compress_sample(ptab, row_pages(cmp_k), row_pages(cmp_v), layer, cw)
    qx = _stack_queries(q, n_s, s_len, KV_A)
    nbl = -(-(-(-(past + s_len) // SEL_BLOCK)) // LANES) * LANES
    o_c, o_w, sel = nsa_sample_small(slopes, qx, kcmp, vcmp, kwt_all, vwt_all, past, win_k.shape[2], nbl)
    o_s = paged_attn(ptab, slopes, qx, sel, _new_rows_t(kv4[:, :, 2]), _new_rows_t(kv4[:, :, 3]),
                     _feature_major(sel_k), _feature_major(sel_v), layer,
                     bs=SEL_BLOCK, tks=min(PAGED_TK, past), past=past, from_means=False)
    o_s_list = [_unstack_outputs(o, n_s, s_len, KV_A) for o in (o_c, o_s, o_w)]
    gate_list = [jnp.repeat(gates[:, br * N_HEADS:(br + 1) * N_HEADS], HEAD_DIM, axis=1) for br in range(3)]
    kv6 = kv.reshape(n_s, s_len, 6, KV_A, HEAD_DIM)
    st_s = tuple(kv6[:, :, i] for i in range(4)) + (st_wk, st_wv)
    return o_t, o_s_list, gate_list, st_p, st_s


def _swa_layer(xp, xs, g_pre, w_in, sinks, layer, caches, past, slopes, n_s):
    buf_k, buf_v = caches
    n_p, t, _ = xp.shape
    s_len = xs.shape[0] // n_s
    gw = KV_B * HEAD_DIM

    q_t, kv_t, kv_tb, k_rows = proj_prompt(xp, g_pre, w_in, (0, 1), False)
    o_t = swa_prompt_attn(slopes, sinks, q_t, k_rows, kv_tb)
    wb = min(WIN_B, t)
    st_p = tuple(_state(kv_t[:, :, t - wb:], i, KV_B) for i in (0, 1))

    q, kv = proj_sample(xs, g_pre, w_in, False)
    kv4 = kv.reshape(n_s, s_len, 2, gw)
    kt_all, st_k = _window_with_new(_feature_major(buf_k)[layer], kv4[:, :, 0], KV_B)
    vt_all, st_v = _window_with_new(_feature_major(buf_v)[layer], kv4[:, :, 1], KV_B)
    o = swa_sample_attn(slopes, sinks, _stack_queries(q, n_s, s_len, KV_B), kt_all, vt_all, past, buf_k.shape[2])
    return o_t, [_unstack_outputs(o, n_s, s_len, KV_B)], st_p, (st_k, st_v)


def _split_hi_lo(x):
    hi = x.astype(BF16)
    return hi, (x - hi.astype(F32)).astype(BF16)


def _moba_layer(xp, xs, g_pre, w_in, layer, caches, ptab, slopes):
    pool_k, pool_v = caches
    n_p, t, _ = xp.shape
    n_s = ptab.shape[0]
    s_len = xs.shape[0] // n_s
    gw = KV_C * HEAD_DIM
    past = ptab.shape[1] * PAGE

    q_t, kv_t, kv_tb, k_rows = proj_prompt(xp, g_pre, w_in, tuple(range(KV_C)), False)
    kmean = block_means_prompt(kv_t).reshape(n_p, KV_C, HEAD_DIM, LANES).transpose(0, 1, 3, 2)
    kmh, kml = _split_hi_lo(kmean)
    o_t = moba_prompt_attn(slopes, q_t, k_rows, kv_tb, kmh, kml)
    st_p = (_state(kv_t, 0, KV_C), _state(kv_t, 1, KV_C))

    q, kv = proj_sample(xs, g_pre, w_in, False)
    kv4 = kv.reshape(n_s, s_len, 2, gw)
    pk_t, pv_t = _feature_major(pool_k), _feature_major(pool_v)
    nc = 2 if past * gw * 4 > (4 << 20) else 1
    kmean = block_means_sample(ptab, pk_t, layer, nc)
    o = paged_attn(ptab, slopes, _stack_queries(q, n_s, s_len, KV_C), kmean,
                   _new_rows_t(kv4[:, :, 0]), _new_rows_t(kv4[:, :, 1]), pk_t, pv_t, layer,
                   bs=MOBA_BLOCK, tks=min(PAGED_TK, past), past=past, from_means=True)
    kv5 = kv.reshape(n_s, s_len, 2, KV_C, HEAD_DIM)
    return o_t, [_unstack_outputs(o, n_s, s_len, KV_C)], st_p, (kv5[:, :, 0], kv5[:, :, 1])


def kernel(x_prompt, x_sample, cache_nsa_cmp_k, cache_nsa_cmp_v, cache_nsa_sel_k, cache_nsa_sel_v, cache_nsa_win_k, cache_nsa_win_v, cache_swa_k, cache_swa_v, cache_moba_k, cache_moba_v, page_table, norm_pre, norm_post, ffn_w_gate, ffn_w_up, ffn_w_down, nsa_w_in, nsa_cmp_pos_k, nsa_cmp_w1_k, nsa_cmp_w2_k, nsa_cmp_pos_v, nsa_cmp_w1_v, nsa_cmp_w2_v, nsa_w_out, swa_w_in, swa_sinks, swa_w_out, moba_w_in, moba_w_out):
    n_p, t, d = x_prompt.shape
    n_s, s_len, _ = x_sample.shape
    depth = norm_pre.shape[0]
    slopes = jnp.exp2(-8.0 * jnp.arange(1, N_HEADS + 1, dtype=F32) / N_HEADS)
    past = page_table.shape[1] * PAGE
    xp = x_prompt.reshape(n_p * t, d)
    xs = x_sample.reshape(n_s * s_len, d)
    nsa_p, nsa_s, swa_p, swa_s, moba_p, moba_s = [], [], [], [], [], []

    def ffn_both(xp, xs, i, which, slot):
        wg, wu, wd = (w[i, which].astype(BF16) for w in (ffn_w_gate, ffn_w_up, ffn_w_down))
        return (ffn(xp, norm_pre[i, slot], norm_post[i, slot], wg, wu, wd),
                ffn(xs, norm_pre[i, slot], norm_post[i, slot], wg, wu, wd))

    for i in range(depth):
        kind, j = i % 3, i // 3
        xp, xs = ffn_both(xp, xs, i, 0, 0)
        xp3 = xp.reshape(n_p, t, d)
        gates_s = None
        if kind == 0:
            cw = {"k": _compress_weights(nsa_cmp_pos_k[j], nsa_cmp_w1_k[j], nsa_cmp_w2_k[j]),
                  "v": _compress_weights(nsa_cmp_pos_v[j], nsa_cmp_w1_v[j], nsa_cmp_w2_v[j])}
            o_t, o_s, gates_s, st_p, st_s = _nsa_layer(
                xp3, xs, norm_pre[i, 1], nsa_w_in[j], cw, j,
                (cache_nsa_cmp_k, cache_nsa_cmp_v, cache_nsa_sel_k, cache_nsa_sel_v, cache_nsa_win_k, cache_nsa_win_v),
                page_table, slopes)
            nsa_p.append(st_p)
            nsa_s.append(st_s)
            w_out = nsa_w_out[j]
        elif kind == 1:
            o_t, o_s, st_p, st_s = _swa_layer(xp3, xs, norm_pre[i, 1], swa_w_in[j], swa_sinks[j], j,
                                              (cache_swa_k, cache_swa_v), past, slopes, n_s)
            swa_p.append(st_p)
            swa_s.append(st_s)
            w_out = swa_w_out[j]
        else:
            o_t, o_s, st_p, st_s = _moba_layer(xp3, xs, norm_pre[i, 1], moba_w_in[j], j,
                                               (cache_moba_k, cache_moba_v), page_table, slopes)
            moba_p.append(st_p)
            moba_s.append(st_s)
            w_out = moba_w_out[j]
        w_out = w_out.astype(BF16)
        xp = out_proj_prompt(o_t, w_out, norm_post[i, 1], xp3).reshape(n_p * t, d)
        xs = out_proj(o_s, w_out, norm_post[i, 1], xs, gates_s)
        xp, xs = ffn_both(xp, xs, i, 1, 2)

    stack = lambda states: [jnp.stack(a) for a in zip(*states)]
    return (xp.reshape(n_p, t, d), xs.reshape(n_s, s_len, d),
            *stack(nsa_p), *stack(swa_p), *stack(moba_p), *stack(nsa_s), *stack(swa_s), *stack(moba_s))
```

```python
import functools

import jax
import jax.numpy as jnp
from jax import lax
from jax.experimental import pallas as pl
from jax.experimental.pallas import tpu as pltpu

F32 = jnp.float32
BF16 = jnp.bfloat16
I32 = jnp.int32

D_MODEL = 1024
N_HEADS = 16
HEAD_DIM = 64
Q_WIDTH = N_HEADS * HEAD_DIM
RMS_EPS = 1e-6
ATTN_SCALE = HEAD_DIM ** -0.5
KV_A, KV_B, KV_C = 2, 2, 4
CMP_LEN, CMP_STRIDE, CMP_HID = 32, 16, 256
SEL_BLOCK, SEL_TOPK = 64, 16
WIN_A, WIN_B = 512, 128
MOBA_BLOCK, MOBA_TOPK = 256, 3
PAGE = 128

LANES = 128
TINY = float(jnp.finfo(jnp.float32).tiny)
M_INIT = -1e30
VMEM_LIMIT = 56 * 1024 * 1024

TQ = 128
FFN_TM, FFN_TF = 1024, 256
PROJ_TM = 512
PAGED_TK = 2048
FLASH_GROUP = 4
assert WIN_A % TQ == 0 and WIN_B % TQ == 0 and MOBA_BLOCK % TQ == 0 and PAGED_TK % MOBA_BLOCK == 0


def _cparams(sem):
    return pltpu.CompilerParams(dimension_semantics=sem, vmem_limit_bytes=VMEM_LIMIT)


def _mm_nt(a, b):
    return lax.dot_general(a, b, (((1,), (1,)), ((), ())), preferred_element_type=F32)


def _mm(a, b):
    return jnp.dot(a, b, preferred_element_type=F32)


def _rms(x, g):
    return x * lax.rsqrt(jnp.mean(x * x, axis=-1, keepdims=True) + RMS_EPS) * g


def _ffn_kernel(x_ref, gpre_ref, gpost_ref, wg_ref, wu_ref, wd_ref, o_ref, xn_ref, acc_ref):
    f = pl.program_id(1)

    @pl.when(f == 0)
    def _():
        xn_ref[...] = _rms(x_ref[...], gpre_ref[...]).astype(BF16)
        acc_ref[...] = jnp.zeros_like(acc_ref)

    xn = xn_ref[...]
    g = _mm(xn, wg_ref[...])
    u = _mm(xn, wu_ref[...])
    h = g * jax.nn.sigmoid(g) * u
    acc_ref[...] += _mm(h.astype(BF16), wd_ref[...])

    @pl.when(f == pl.num_programs(1) - 1)
    def _():
        o_ref[...] = x_ref[...] + 0.5 * _rms(acc_ref[...], gpost_ref[...])


def ffn(x, g_pre, g_post, wg, wu, wd):
    m, d = x.shape
    dff = wg.shape[1]
    tm = min(FFN_TM, m)
    return pl.pallas_call(
        _ffn_kernel,
        grid=(m // tm, dff // FFN_TF),
        in_specs=[
            pl.BlockSpec((tm, d), lambda i, f: (i, 0)),
            pl.BlockSpec((1, d), lambda i, f: (0, 0)),
            pl.BlockSpec((1, d), lambda i, f: (0, 0)),
            pl.BlockSpec((d, FFN_TF), lambda i, f: (0, f)),
            pl.BlockSpec((d, FFN_TF), lambda i, f: (0, f)),
            pl.BlockSpec((FFN_TF, d), lambda i, f: (f, 0)),
        ],
        out_specs=pl.BlockSpec((tm, d), lambda i, f: (i, 0)),
        out_shape=jax.ShapeDtypeStruct((m, d), F32),
        scratch_shapes=[pltpu.VMEM((tm, d), BF16), pltpu.VMEM((tm, d), F32)],
        compiler_params=_cparams(("parallel", "arbitrary")),
        name="ffn",
    )(x, g_pre.reshape(1, d), g_post.reshape(1, d), wg, wu, wd)


def _proj_prompt_kernel(x_ref, g_ref, wqt_ref, wkvt_ref, wk_ref, *rest, nsa):
    if nsa:
        wc_ref, wgt_ref, qt_ref, kvt_ref, kvtb_ref, k_ref, c_ref, gt_ref = rest
    else:
        qt_ref, kvt_ref, kvtb_ref, k_ref = rest
    xn = _rms(x_ref[0], g_ref[...]).astype(BF16)
    qt_ref[0] = (_mm_nt(wqt_ref[...], xn) * ATTN_SCALE).astype(BF16)
    kvt = _mm_nt(wkvt_ref[...], xn)
    kvt_ref[0] = kvt
    kvtb_ref[0] = kvt.astype(BF16)
    for c in range(wk_ref.shape[0]):
        k_ref[0, c] = _mm(xn, wk_ref[c]).astype(BF16)
    if nsa:
        c_ref[0] = _mm(xn, wc_ref[...])
        gt_ref[0] = jax.nn.sigmoid(_mm_nt(wgt_ref[...], xn))


def proj_prompt(x, g, w_in, k_cols, nsa):
    n, t, d = x.shape
    kvw = w_in.shape[1] - Q_WIDTH - (3 * N_HEADS if nsa else 0)
    wqt = w_in[:, :Q_WIDTH].T.astype(BF16)
    wkv = w_in[:, Q_WIDTH:Q_WIDTH + kvw]
    wkvt = wkv.T.astype(BF16)
    wk = jnp.stack([wkv[:, c * HEAD_DIM:(c + 1) * HEAD_DIM] for c in k_cols]).astype(BF16)
    ck = len(k_cols)
    fixed2 = lambda i, j: (0, 0)
    fixed3 = lambda i, j: (0, 0, 0)
    in_specs = [pl.BlockSpec((1, TQ, d), lambda i, j: (i, j, 0)), pl.BlockSpec((1, d), fixed2),
                pl.BlockSpec((Q_WIDTH, d), fixed2), pl.BlockSpec((kvw, d), fixed2),
                pl.BlockSpec((ck, d, HEAD_DIM), fixed3)]
    out_specs = [pl.BlockSpec((1, Q_WIDTH, TQ), lambda i, j: (i, 0, j)),
                 pl.BlockSpec((1, kvw, TQ), lambda i, j: (i, 0, j)),
                 pl.BlockSpec((1, kvw, TQ), lambda i, j: (i, 0, j)),
                 pl.BlockSpec((1, ck, TQ, HEAD_DIM), lambda i, j: (i, 0, j, 0))]
    out_shape = [jax.ShapeDtypeStruct((n, Q_WIDTH, t), BF16),
                 jax.ShapeDtypeStruct((n, kvw, t), F32), jax.ShapeDtypeStruct((n, kvw, t), BF16),
                 jax.ShapeDtypeStruct((n, ck, t, HEAD_DIM), BF16)]
    args = [x, g.reshape(1, d), wqt, wkvt, wk]
    if nsa:
        wgt = jnp.pad(w_in[:, Q_WIDTH + kvw:], ((0, 0), (0, LANES - 3 * N_HEADS))).T.astype(BF16)
        in_specs += [pl.BlockSpec((d, 2 * LANES), fixed2), pl.BlockSpec((LANES, d), fixed2)]
        out_specs += [pl.BlockSpec((1, TQ, 2 * LANES), lambda i, j: (i, j, 0)),
                      pl.BlockSpec((1, LANES, TQ), lambda i, j: (i, 0, j))]
        out_shape += [jax.ShapeDtypeStruct((n, t, 2 * LANES), F32), jax.ShapeDtypeStruct((n, LANES, t), F32)]
        args += [wkv[:, :2 * LANES].astype(BF16), wgt]
    return pl.pallas_call(
        functools.partial(_proj_prompt_kernel, nsa=nsa),
        grid=(n, t // TQ), in_specs=in_specs, out_specs=out_specs, out_shape=out_shape,
        compiler_params=_cparams(("parallel", "parallel")), name="proj_prompt",
    )(*args)


def _proj_kernel(x_ref, g_ref, wq_ref, wkv_ref, *rest, has_gates):
    if has_gates:
        wg_ref, q_ref, kv_ref, gate_ref = rest
    else:
        q_ref, kv_ref = rest
    xn = _rms(x_ref[...], g_ref[...]).astype(BF16)
    q_ref[...] = (_mm(xn, wq_ref[...]) * ATTN_SCALE).astype(BF16)
    kv_ref[...] = _mm(xn, wkv_ref[...])
    if has_gates:
        gate_ref[...] = jax.nn.sigmoid(_mm(xn, wg_ref[...]))


def proj_sample(x, g, w_in, nsa):
    m, d = x.shape
    kvw = w_in.shape[1] - Q_WIDTH - (3 * N_HEADS if nsa else 0)
    tm = min(PROJ_TM, m)
    row = lambda i: (i, 0)
    fixed = lambda i: (0, 0)
    in_specs = [pl.BlockSpec((tm, d), row), pl.BlockSpec((1, d), fixed),
                pl.BlockSpec((d, Q_WIDTH), fixed), pl.BlockSpec((d, kvw), fixed)]
    out_specs = [pl.BlockSpec((tm, Q_WIDTH), row), pl.BlockSpec((tm, kvw), row)]
    out_shape = [jax.ShapeDtypeStruct((m, Q_WIDTH), BF16), jax.ShapeDtypeStruct((m, kvw), F32)]
    args = [x, g.reshape(1, d), w_in[:, :Q_WIDTH].astype(BF16), w_in[:, Q_WIDTH:Q_WIDTH + kvw].astype(BF16)]
    if nsa:
        in_specs.append(pl.BlockSpec((d, LANES), fixed))
        out_specs.append(pl.BlockSpec((tm, LANES), row))
        out_shape.append(jax.ShapeDtypeStruct((m, LANES), F32))
        args.append(jnp.pad(w_in[:, Q_WIDTH + kvw:], ((0, 0), (0, LANES - 3 * N_HEADS))).astype(BF16))
    return pl.pallas_call(
        functools.partial(_proj_kernel, has_gates=nsa),
        grid=(m // tm,), in_specs=in_specs, out_specs=out_specs, out_shape=out_shape,
        compiler_params=_cparams(("parallel",)), name="proj_sample",
    )(*args)


def _out_prompt_kernel(ot_ref, w_ref, gp_ref, x_ref, y_ref):
    mix = lax.dot_general(ot_ref[0], w_ref[...], (((0,), (0,)), ((), ())), preferred_element_type=F32)
    y_ref[0] = x_ref[0] + _rms(mix, gp_ref[...])


def out_proj_prompt(o_t, w_out, g_post, x):
    n, t, d = x.shape
    return pl.pallas_call(
        _out_prompt_kernel,
        grid=(n, t // TQ),
        in_specs=[pl.BlockSpec((1, Q_WIDTH, TQ), lambda i, j: (i, 0, j)),
                  pl.BlockSpec((Q_WIDTH, d), lambda i, j: (0, 0)),
                  pl.BlockSpec((1, d), lambda i, j: (0, 0)),
                  pl.BlockSpec((1, TQ, d), lambda i, j: (i, j, 0))],
        out_specs=pl.BlockSpec((1, TQ, d), lambda i, j: (i, j, 0)),
        out_shape=jax.ShapeDtypeStruct((n, t, d), F32),
        compiler_params=_cparams(("parallel", "parallel")), name="out_proj_prompt",
    )(o_t, w_out, g_post.reshape(1, d), x)


def _out_kernel(*refs, n_o, gated):
    o_refs = refs[:n_o]
    pos = n_o
    if gated:
        g_refs = refs[pos:pos + n_o]
        pos += n_o
    w_ref, gp_ref, x_ref, y_ref = refs[pos:pos + 4]
    o = None
    for i in range(n_o):
        t = o_refs[i][...]
        if gated:
            t = g_refs[i][...] * t
        o = t if o is None else o + t
    y_ref[...] = x_ref[...] + _rms(_mm(o.astype(BF16), w_ref[...]), gp_ref[...])


def out_proj(o_list, w_out, g_post, x, gate_list=None):
    m, d = x.shape
    tm = min(PROJ_TM, m)
    n_o = len(o_list)
    gated = gate_list is not None
    row = lambda i: (i, 0)
    fixed = lambda i: (0, 0)
    in_specs = [pl.BlockSpec((tm, Q_WIDTH), row)] * (n_o * (2 if gated else 1))
    in_specs += [pl.BlockSpec((Q_WIDTH, d), fixed), pl.BlockSpec((1, d), fixed), pl.BlockSpec((tm, d), row)]
    args = list(o_list) + (list(gate_list) if gated else []) + [w_out, g_post.reshape(1, d), x]
    return pl.pallas_call(
        functools.partial(_out_kernel, n_o=n_o, gated=gated),
        grid=(m // tm,), in_specs=in_specs, out_specs=pl.BlockSpec((tm, d), row),
        out_shape=jax.ShapeDtypeStruct((m, d), F32),
        compiler_params=_cparams(("parallel",)), name="out_proj",
    )(*args)


def _col(ref, heads, rows):
    return jnp.concatenate([jnp.full((rows, 1), ref[h], F32) for h in heads], axis=0)


def _row(ref, heads, cols):
    return jnp.concatenate([jnp.full((1, cols), ref[h], F32) for h in heads], axis=1)


def _group_queries(qt_ref, g, hg):
    base = g * hg * HEAD_DIM
    return jnp.concatenate([qt_ref[0, base + h * HEAD_DIM:base + (h + 1) * HEAD_DIM, :] for h in range(hg)], axis=1)


def _store_group(ot_ref, g, hg, o):
    base = g * hg * HEAD_DIM
    for h in range(hg):
        ot_ref[0, base + h * HEAD_DIM:base + (h + 1) * HEAD_DIM, :] = o[:, h * TQ:(h + 1) * TQ].astype(BF16)


def _expand_blocks(sel, key0, tk, bs):
    nb = sel.shape[1]
    blk = (key0 + lax.broadcasted_iota(I32, (nb, tk), 1)) // bs
    onehot = (blk == lax.broadcasted_iota(I32, (nb, tk), 0)).astype(BF16)
    return _mm(sel, onehot)


def _expand_blocks_t(sel_t, key0, tk, bs):
    nb = sel_t.shape[0]
    blk = (key0 + lax.broadcasted_iota(I32, (tk, nb), 0)) // bs
    onehot = (blk == lax.broadcasted_iota(I32, (tk, nb), 1)).astype(BF16)
    return _mm(onehot, sel_t)


def _topk_mask(sc, nb, topk):
    jidx = lax.broadcasted_iota(I32, sc.shape, 1)
    rank = jnp.zeros(sc.shape, I32)
    for jp in range(nb):
        col = sc[:, jp:jp + 1]
        ahead = (col > sc) | ((col == sc) & (jidx > jp))
        rank = rank + ahead.astype(I32)
    return rank < topk


def _topk_mask_t(sc, nb, topk):
    jidx = lax.broadcasted_iota(I32, sc.shape, 0)
    rank = jnp.zeros(sc.shape, I32)
    for jp in range(nb):
        row = sc[jp:jp + 1, :]
        ahead = (row > sc) | ((row == sc) & (jidx > jp))
        rank = rank + ahead.astype(I32)
    return rank < topk


def _alibi_rise(slopes, groups, n):
    hg = N_HEADS // groups
    lane_slope = jnp.repeat(slopes.reshape(groups, hg), TQ, axis=1)
    return lane_slope[:, None, :] * jnp.arange(n, dtype=F32)[None, :, None]


def _flash_t(qt, k_at, vt_at, segments, tk, qpos, slope, rise_at, m0, l0, m_ref, l_ref, acc_ref):
    m_ref[...] = m0
    l_ref[...] = l0
    acc_ref[...] = jnp.zeros_like(acc_ref)
    cols = qt.shape[1]
    def body(i, carry, first, n, mask_fn):
        key0 = (first + i * (n // tk)) * tk
        ahead = qpos - key0
        s = _mm(k_at(key0, n), qt) + rise_at(n) - slope * ahead.astype(F32)
        if mask_fn is not None:
            s = jnp.where(mask_fn(key0, n, lambda: ahead - lax.broadcasted_iota(I32, (n, cols), 0)), s, -jnp.inf)
        m_old = m_ref[...]
        m_new = jnp.maximum(m_old, jnp.max(s, axis=0, keepdims=True))
        alpha = jnp.exp(m_old - m_new)
        p = jnp.exp(s - m_new)
        l_ref[...] = alpha * l_ref[...] + jnp.sum(p, axis=0, keepdims=True)
        acc_ref[...] = alpha * acc_ref[...] + _mm(vt_at(key0, n), p.astype(BF16))
        m_ref[...] = m_new
        return carry

    for lo, hi, mask_fn, group in segments:
        if group > 1:
            steps = (hi - lo) // group
            lax.fori_loop(0, steps, functools.partial(body, first=lo, n=group * tk, mask_fn=mask_fn), 0)
            lo = lo + steps * group
        lax.fori_loop(0, hi - lo, functools.partial(body, first=lo, n=tk, mask_fn=mask_fn), 0)
    return acc_ref[...] / jnp.maximum(l_ref[...], TINY)


def _softmax_rows(s, sink=None):
    m = jnp.max(s, axis=1, keepdims=True)
    if sink is not None:
        m = jnp.maximum(m, sink)
    m = jnp.where(m > -jnp.inf, m, 0.0)
    e = jnp.exp(s - m)
    den = jnp.sum(e, axis=1, keepdims=True)
    if sink is not None:
        den = den + jnp.exp(sink - m)
    return e / jnp.maximum(den, TINY)


def _softmax_cols(s):
    m = jnp.max(s, axis=0, keepdims=True)
    m = jnp.where(m > -jnp.inf, m, 0.0)
    e = jnp.exp(s - m)
    return e / jnp.maximum(jnp.sum(e, axis=0, keepdims=True), TINY)


def _compress_rows(load_rows, nch, wbig_ref, pos_ref, w2_ref):
    chunks = jnp.concatenate([load_rows(l).astype(BF16) for l in range(CMP_STRIDE)], axis=1)
    acc = _mm(chunks, wbig_ref[...])
    first = acc[:, :2 * CMP_HID]
    second = pltpu.roll(acc[:, 2 * CMP_HID:], nch - 1, 0)
    hid = first + second + pos_ref[0:1, :]
    return _mm(jax.nn.gelu(hid).astype(BF16), w2_ref[...])


def _compress_prompt_kernel(k_ref, v_ref, wk_ref, pk_ref, w2k_ref, wv_ref, pv_ref, w2v_ref, ok_ref, ov_ref, *, nch):
    for x_ref, w, p, w2, o_ref in ((k_ref, wk_ref, pk_ref, w2k_ref, ok_ref), (v_ref, wv_ref, pv_ref, w2v_ref, ov_ref)):
        load = lambda l, x_ref=x_ref: x_ref[0, pl.ds(l, nch, stride=CMP_STRIDE), :]
        o_ref[0] = _compress_rows(load, nch, w, p, w2).astype(BF16)


def compress_prompt(rows, cw):
    n, t, _ = rows.shape
    nch = t // CMP_STRIDE
    wspec = [pl.BlockSpec((CMP_STRIDE * LANES, 4 * CMP_HID), lambda i: (0, 0)),
             pl.BlockSpec((8, 2 * CMP_HID), lambda i: (0, 0)),
             pl.BlockSpec((2 * CMP_HID, LANES), lambda i: (0, 0))]
    out = jax.ShapeDtypeStruct((n, nch, LANES), BF16)
    return pl.pallas_call(
        functools.partial(_compress_prompt_kernel, nch=nch),
        grid=(n,),
        in_specs=[pl.BlockSpec((1, t, LANES), lambda i: (i, 0, 0)),
                  pl.BlockSpec((1, t, LANES), lambda i: (i, 0, 1))] + wspec + wspec,
        out_specs=[pl.BlockSpec((1, nch, LANES), lambda i: (i, 0, 0))] * 2,
        out_shape=[out, out],
        compiler_params=_cparams(("parallel",)), name="compress_prompt",
    )(rows, rows, *cw["k"], *cw["v"])


def _pos_term_kernel(p_ref, w_ref, o_ref):
    o_ref[...] = jnp.dot(p_ref[...], w_ref[...], precision=lax.Precision.HIGHEST, preferred_element_type=F32)


def _compress_weights(pos, w1, w2):
    eye = jnp.eye(KV_A, dtype=F32)
    w1r = w1.reshape(CMP_LEN // CMP_STRIDE, CMP_STRIDE, HEAD_DIM, CMP_HID)
    wbig = jnp.einsum("jldh,ab->lbdjah", w1r, eye).reshape(CMP_STRIDE * LANES, 4 * CMP_HID).astype(BF16)
    w2big = jnp.einsum("hd,ab->ahbd", w2, eye).reshape(KV_A * CMP_HID, LANES).astype(BF16)
    pos8 = jnp.broadcast_to(pos.reshape(1, CMP_LEN * HEAD_DIM), (8, CMP_LEN * HEAD_DIM))
    pterm = pl.pallas_call(
        _pos_term_kernel, out_shape=jax.ShapeDtypeStruct((8, CMP_HID), F32), name="cmp_pos_term",
    )(pos8, w1)
    return wbig, jnp.concatenate([pterm, pterm], axis=1), w2big


def _page_copies(ptab_ref, pools, bufs, sem, layer, n, c, slot, pps, feature_major):
    copies = []
    for i in range(pps):
        page = ptab_ref[n, c * pps + i]
        for j, (pool, buf) in enumerate(zip(pools, bufs)):
            dst = buf.at[slot, :, pl.ds(i * PAGE, PAGE)] if feature_major else buf.at[slot, pl.ds(i * PAGE, PAGE)]
            copies.append(pltpu.make_async_copy(pool.at[layer, page], dst, sem.at[j, slot]))
    return copies


def _paged_pipeline(ptab_ref, pools, bufs, sem, layer, nc, pps, feature_major):
    n, c = pl.program_id(0), pl.program_id(1)
    step = n * nc + c
    slot = step % 2
    copies = functools.partial(_page_copies, ptab_ref, pools, bufs, sem, layer, pps=pps, feature_major=feature_major)

    @pl.when(step == 0)
    def _():
        for cp in copies(n, c, slot):
            cp.start()

    @pl.when(step + 1 < pl.num_programs(0) * nc)
    def _():
        nxt = step + 1
        for cp in copies(nxt // nc, nxt % nc, 1 - slot):
            cp.start()

    for cp in copies(n, c, slot):
        cp.wait()
    return slot


def _compress_sample_kernel(ptab_ref, kpool, vpool, wk_ref, pk_ref, w2k_ref, wv_ref, pv_ref, w2v_ref,
                            ok_ref, ov_ref, kbuf, vbuf, sem, rows_ref, *, layer, n_pages):
    slot = _paged_pipeline(ptab_ref, (kpool, vpool), (kbuf, vbuf), sem, layer, 1, n_pages, True)
    nch = n_pages * PAGE // CMP_STRIDE
    for buf, w, p, w2, o_ref in ((kbuf, wk_ref, pk_ref, w2k_ref, ok_ref), (vbuf, wv_ref, pv_ref, w2v_ref, ov_ref)):
        for i in range(n_pages):
            rows_ref[i * PAGE:(i + 1) * PAGE, :] = buf[slot, :, i * PAGE:(i + 1) * PAGE].T
        load = lambda l: rows_ref[pl.ds(l, nch, stride=CMP_STRIDE), :]
        o_ref[0] = _compress_rows(load, nch, w, p, w2).astype(BF16)


def compress_sample(ptab, pool_k, pool_v, layer, cw):
    n, n_pages = ptab.shape
    rows = n_pages * PAGE
    nch = rows // CMP_STRIDE
    wspec = [pl.BlockSpec((CMP_STRIDE * LANES, 4 * CMP_HID), lambda i, c, pt: (0, 0)),
             pl.BlockSpec((8, 2 * CMP_HID), lambda i, c, pt: (0, 0)),
             pl.BlockSpec((2 * CMP_HID, LANES), lambda i, c, pt: (0, 0))]
    anyspec = pl.BlockSpec(memory_space=pl.ANY)
    out = jax.ShapeDtypeStruct((n, nch, LANES), BF16)
    return pl.pallas_call(
        functools.partial(_compress_sample_kernel, layer=layer, n_pages=n_pages),
        grid_spec=pltpu.PrefetchScalarGridSpec(
            num_scalar_prefetch=1, grid=(n, 1),
            in_specs=[anyspec, anyspec] + wspec + wspec,
            out_specs=[pl.BlockSpec((1, nch, LANES), lambda i, c, pt: (i, 0, 0))] * 2,
            scratch_shapes=[pltpu.VMEM((2, LANES, rows), F32), pltpu.VMEM((2, LANES, rows), F32),
                            pltpu.SemaphoreType.DMA((2, 2)), pltpu.VMEM((rows, LANES), F32)]),
        out_shape=[out, out],
        compiler_params=_cparams(("arbitrary", "arbitrary")), name="compress_sample",
    )(ptab, pool_k, pool_v, *cw["k"], *cw["v"])


def _nsa_prompt_kernel(slopes_ref, qt_ref, k_ref, kvt_ref, kc_ref, vct_ref, gt_ref, rise_ref, ot_ref,
                       m_ref, l_ref, acc_ref, *, t_len):
    qi = pl.program_id(1)
    q0 = qi * TQ
    hg = N_HEADS // KV_A
    cols = hg * TQ
    qpos = q0 + (lax.broadcasted_iota(I32, (1, cols), 1) & (TQ - 1))
    qpos_t = q0 + lax.broadcasted_iota(I32, (1, TQ), 1)
    ncp = kc_ref.shape[2]
    n_cmp = t_len // CMP_STRIDE - CMP_LEN // CMP_STRIDE + 1
    n_sel = t_len // SEL_BLOCK
    m0 = jnp.full((1, cols), M_INIT, F32)
    l0 = jnp.zeros((1, cols), F32)
    gw = KV_A * HEAD_DIM

    for g in range(KV_A):
        heads = range(g * hg, (g + 1) * hg)
        qt = _group_queries(qt_ref, g, hg)
        slope = _row(slopes_ref, heads, TQ)

        cidx = lax.broadcasted_iota(I32, (ncp, cols), 0)
        dist = qpos - (cidx * CMP_STRIDE + CMP_LEN - 1)
        s = _mm(kc_ref[0, g], qt)
        s = jnp.where((dist >= 0) & (cidx < n_cmp), s - slope * dist.astype(F32), -jnp.inf)
        p = _softmax_cols(s)
        o_c = _mm(vct_ref[0, g], p.astype(BF16))
        imp = p[:, 0:TQ]
        for h in range(1, hg):
            imp = imp + p[:, h * TQ:(h + 1) * TQ]

        srow = lax.broadcasted_iota(I32, (LANES, ncp), 0) * SEL_BLOCK
        ccol = lax.broadcasted_iota(I32, (LANES, ncp), 1) * CMP_STRIDE
        overlap = ((ccol < srow + SEL_BLOCK) & (ccol + CMP_LEN > srow)).astype(F32)
        score = jnp.dot(overlap, imp, precision=lax.Precision.HIGHEST, preferred_element_type=F32)
        jidx = lax.broadcasted_iota(I32, (LANES, TQ), 0)
        qb = qpos_t // SEL_BLOCK
        forced = (jidx == 0) | (jidx == qb) | (jidx == qb - 1)
        causal = jidx <= qb
        sc = jnp.where(forced, jnp.inf, jnp.where(causal, score, -jnp.inf))
        sel = (_topk_mask_t(sc, n_sel, SEL_TOPK) & causal).astype(BF16)

        def picked(key0, n, dist, sel=sel):
            e = _expand_blocks_t(sel, key0, n, SEL_BLOCK)
            return jnp.concatenate([e] * hg, axis=1) > 0.5

        keys = lambda key0, n: pl.ds(pl.multiple_of(key0, TQ), n)
        o_s = _flash_t(qt, lambda key0, n, g=g: k_ref[0, g, keys(key0, n), :],
                       lambda key0, n, g=g: kvt_ref[0, 3 * gw + g * HEAD_DIM:3 * gw + (g + 1) * HEAD_DIM, keys(key0, n)],
                       [(0, qi, picked, FLASH_GROUP),
                        (qi, qi + 1, lambda key0, n, dist: picked(key0, n, dist) & (dist() >= 0), 1)],
                       TQ, qpos, slope, lambda n, g=g: rise_ref[g, :n, :], m0, l0, m_ref, l_ref, acc_ref)

        back = -(-(WIN_A - 1) // TQ)
        o_w = _flash_t(qt, lambda key0, n, g=g: k_ref[0, KV_A + g, keys(key0, n), :],
                       lambda key0, n, g=g: kvt_ref[0, 5 * gw + g * HEAD_DIM:5 * gw + (g + 1) * HEAD_DIM, keys(key0, n)],
                       [(jnp.maximum(qi - back, 0), jnp.maximum(qi - back + 1, 0),
                         lambda key0, n, dist: dist() < WIN_A, 1),
                        (jnp.maximum(qi - back + 1, 0), qi, None, back - 1),
                        (qi, qi + 1, lambda key0, n, dist: dist() >= 0, 1)],
                       TQ, qpos, slope, lambda n, g=g: rise_ref[g, :n, :], m0, l0, m_ref, l_ref, acc_ref)

        gates = gt_ref[0]
        grow = lambda br: jnp.concatenate([gates[br * N_HEADS + h:br * N_HEADS + h + 1, :] for h in heads], axis=1)
        _store_group(ot_ref, g, hg, grow(0) * o_c + grow(1) * o_s + grow(2) * o_w)


def nsa_prompt_attn(slopes, q_t, k_rows, kv_t, kc, vc_t, gates_t):
    n, _, t = q_t.shape
    hg = N_HEADS // KV_A
    ncp = kc.shape[2]
    cols = hg * TQ
    return pl.pallas_call(
        functools.partial(_nsa_prompt_kernel, t_len=t),
        grid_spec=pltpu.PrefetchScalarGridSpec(
            num_scalar_prefetch=1, grid=(n, t // TQ),
            in_specs=[pl.BlockSpec((1, Q_WIDTH, TQ), lambda i, j, s: (i, 0, j)),
                      pl.BlockSpec((1, 2 * KV_A, t, HEAD_DIM), lambda i, j, s: (i, 0, 0, 0)),
                      pl.BlockSpec((1, kv_t.shape[1], t), lambda i, j, s: (i, 0, 0)),
                      pl.BlockSpec((1, KV_A, ncp, HEAD_DIM), lambda i, j, s: (i, 0, 0, 0)),
                      pl.BlockSpec((1, KV_A, HEAD_DIM, ncp), lambda i, j, s: (i, 0, 0, 0)),
                      pl.BlockSpec((1, LANES, TQ), lambda i, j, s: (i, 0, j)),
                      pl.BlockSpec((KV_A, FLASH_GROUP * TQ, cols), lambda i, j, s: (0, 0, 0))],
            out_specs=pl.BlockSpec((1, Q_WIDTH, TQ), lambda i, j, s: (i, 0, j)),
            scratch_shapes=[pltpu.VMEM((1, cols), F32), pltpu.VMEM((1, cols), F32),
                            pltpu.VMEM((HEAD_DIM, cols), F32)]),
        out_shape=jax.ShapeDtypeStruct((n, Q_WIDTH, t), BF16),
        compiler_params=_cparams(("parallel", "arbitrary")), name="nsa_prompt_attn",
    )(slopes, q_t, k_rows, kv_t, kc, vc_t, gates_t, _alibi_rise(slopes, KV_A, FLASH_GROUP * TQ))


def _swa_prompt_kernel(slopes_ref, sinks_ref, qt_ref, k_ref, kvt_ref, rise_ref, ot_ref, m_ref, l_ref, acc_ref):
    qi = pl.program_id(1)
    hg = N_HEADS // KV_B
    cols = hg * TQ
    gw = KV_B * HEAD_DIM
    qpos = qi * TQ + (lax.broadcasted_iota(I32, (1, cols), 1) & (TQ - 1))
    back = -(-(WIN_B - 1) // TQ)
    segments = [(jnp.maximum(qi - back, 0), qi, lambda key0, n, dist: dist() < WIN_B, 1),
                (qi, qi + 1, lambda key0, n, dist: dist() >= 0, 1)]
    keys = lambda key0, n: pl.ds(pl.multiple_of(key0, TQ), n)
    for g in range(KV_B):
        heads = range(g * hg, (g + 1) * hg)
        o = _flash_t(_group_queries(qt_ref, g, hg), lambda key0, n, g=g: k_ref[0, g, keys(key0, n), :],
                     lambda key0, n, g=g: kvt_ref[0, gw + g * HEAD_DIM:gw + (g + 1) * HEAD_DIM, keys(key0, n)],
                     segments, TQ, qpos, _row(slopes_ref, heads, TQ), lambda n, g=g: rise_ref[g, :n, :],
                     _row(sinks_ref, heads, TQ), jnp.ones((1, cols), F32), m_ref, l_ref, acc_ref)
        _store_group(ot_ref, g, hg, o)


def swa_prompt_attn(slopes, sinks, q_t, k_rows, kv_t):
    n, _, t = q_t.shape
    hg = N_HEADS // KV_B
    cols = hg * TQ
    return pl.pallas_call(
        _swa_prompt_kernel,
        grid_spec=pltpu.PrefetchScalarGridSpec(
            num_scalar_prefetch=2, grid=(n, t // TQ),
            in_specs=[pl.BlockSpec((1, Q_WIDTH, TQ), lambda i, j, s, k: (i, 0, j)),
                      pl.BlockSpec((1, KV_B, t, HEAD_DIM), lambda i, j, s, k: (i, 0, 0, 0)),
                      pl.BlockSpec((1, kv_t.shape[1], t), lambda i, j, s, k: (i, 0, 0)),
                      pl.BlockSpec((KV_B, TQ, cols), lambda i, j, s, k: (0, 0, 0))],
            out_specs=pl.BlockSpec((1, Q_WIDTH, TQ), lambda i, j, s, k: (i, 0, j)),
            scratch_shapes=[pltpu.VMEM((1, cols), F32), pltpu.VMEM((1, cols), F32),
                            pltpu.VMEM((HEAD_DIM, cols), F32)]),
        out_shape=jax.ShapeDtypeStruct((n, Q_WIDTH, t), BF16),
        compiler_params=_cparams(("parallel", "arbitrary")), name="swa_prompt_attn",
    )(slopes, sinks, q_t, k_rows, kv_t, _alibi_rise(slopes, KV_B, TQ))


def _block_mean_prompt_kernel(kt_ref, o_ref, *, nb):
    lane = lax.broadcasted_iota(I32, o_ref.shape[1:], 1)
    out = jnp.zeros(o_ref.shape[1:], F32)
    for b in range(nb):
        blk = kt_ref[0, :, b * MOBA_BLOCK:(b + 1) * MOBA_BLOCK]
        out = jnp.where(lane == b, jnp.sum(blk, axis=1, keepdims=True) / MOBA_BLOCK, out)
    o_ref[0] = out


def block_means_prompt(kv_t):
    n, _, t = kv_t.shape
    w = KV_C * HEAD_DIM
    return pl.pallas_call(
        functools.partial(_block_mean_prompt_kernel, nb=t // MOBA_BLOCK),
        grid=(n,),
        in_specs=[pl.BlockSpec((1, w, t), lambda i: (i, 0, 0))],
        out_specs=pl.BlockSpec((1, w, LANES), lambda i: (i, 0, 0)),
        out_shape=jax.ShapeDtypeStruct((n, w, LANES), F32),
        compiler_params=_cparams(("parallel",)), name="block_means_prompt",
    )(kv_t)


def _moba_prompt_kernel(slopes_ref, qt_ref, k_ref, kvt_ref, kmh_ref, kml_ref, rise_ref, ot_ref,
                        m_ref, l_ref, acc_ref, sel_ref,
                        *, t_len):
    qi = pl.program_id(1)
    q0 = qi * TQ
    hg = N_HEADS // KV_C
    cols = hg * TQ
    tk = MOBA_BLOCK
    gw = KV_C * HEAD_DIM
    nb = t_len // MOBA_BLOCK
    qpos = q0 + (lax.broadcasted_iota(I32, (1, cols), 1) & (TQ - 1))
    m0 = jnp.full((1, cols), M_INIT, F32)
    l0 = jnp.zeros((1, cols), F32)
    bidx = lax.broadcasted_iota(I32, (LANES, cols), 0)
    qb = qpos // MOBA_BLOCK
    keys = lambda key0, n: pl.ds(pl.multiple_of(key0, tk), n)
    for g in range(KV_C):
        heads = range(g * hg, (g + 1) * hg)
        qt = _group_queries(qt_ref, g, hg)
        gate = _mm(kmh_ref[0, g], qt) + _mm(kml_ref[0, g], qt)
        past_blk = bidx < qb
        sc = jnp.where(past_blk, gate, -jnp.inf)
        sel_ref[...] = ((_topk_mask_t(sc, nb, MOBA_TOPK) & past_blk) | (bidx == qb)).astype(F32)

        picked = lambda key0, n, dist: sel_ref[pl.ds(key0 // tk, 1), :] > 0.5
        last = (q0 + TQ - 1) // tk
        o = _flash_t(qt, lambda key0, n, g=g: k_ref[0, g, keys(key0, n), :],
                     lambda key0, n, g=g: kvt_ref[0, gw + g * HEAD_DIM:gw + (g + 1) * HEAD_DIM, keys(key0, n)],
                     [(0, last, picked, 1), (last, last + 1, lambda key0, n, dist: dist() >= 0, 1)],
                     tk, qpos, _row(slopes_ref, heads, TQ), lambda n, g=g: rise_ref[g, :n, :],
                     m0, l0, m_ref, l_ref, acc_ref)
        _store_group(ot_ref, g, hg, o)


def moba_prompt_attn(slopes, q_t, k_rows, kv_t, kmh, kml):
    n, _, t = q_t.shape
    hg = N_HEADS // KV_C
    cols = hg * TQ
    return pl.pallas_call(
        functools.partial(_moba_prompt_kernel, t_len=t),
        grid_spec=pltpu.PrefetchScalarGridSpec(
            num_scalar_prefetch=1, grid=(n, t // TQ),
            in_specs=[pl.BlockSpec((1, Q_WIDTH, TQ), lambda i, j, s: (i, 0, j)),
                      pl.BlockSpec((1, KV_C, t, HEAD_DIM), lambda i, j, s: (i, 0, 0, 0)),
                      pl.BlockSpec((1, kv_t.shape[1], t), lambda i, j, s: (i, 0, 0)),
                      pl.BlockSpec((1, KV_C, LANES, HEAD_DIM), lambda i, j, s: (i, 0, 0, 0)),
                      pl.BlockSpec((1, KV_C, LANES, HEAD_DIM), lambda i, j, s: (i, 0, 0, 0)),
                      pl.BlockSpec((KV_C, MOBA_BLOCK, cols), lambda i, j, s: (0, 0, 0))],
            out_specs=pl.BlockSpec((1, Q_WIDTH, TQ), lambda i, j, s: (i, 0, j)),
            scratch_shapes=[pltpu.VMEM((1, cols), F32), pltpu.VMEM((1, cols), F32),
                            pltpu.VMEM((HEAD_DIM, cols), F32), pltpu.VMEM((LANES, cols), F32)]),
        out_shape=jax.ShapeDtypeStruct((n, Q_WIDTH, t), BF16),
        compiler_params=_cparams(("parallel", "arbitrary")), name="moba_prompt_attn",
    )(slopes, q_t, k_rows, kv_t, kmh, kml, _alibi_rise(slopes, KV_C, MOBA_BLOCK))


def _sample_cols(slopes_ref, s_len, past):
    slope = _col(slopes_ref, range(N_HEADS), s_len)
    qpos = past + (lax.broadcasted_iota(I32, (N_HEADS * s_len, 1), 0) % s_len)
    return slope, qpos


def _window_sample(qx, kt_all, vt_all, qpos, slope, wb, window, past, sink=None):
    kpos = past - wb + lax.broadcasted_iota(I32, (1, kt_all.shape[1]), 1)
    dist = qpos - kpos
    s = _mm(qx, kt_all.astype(BF16))
    s = jnp.where((dist >= 0) & (dist < window) & (kpos >= 0), s - slope * dist.astype(F32), -jnp.inf)
    p = _softmax_rows(s, sink)
    return _mm_nt(p.astype(BF16), vt_all.astype(BF16))


def _nsa_sample_small_kernel(slopes_ref, qx_ref, kc_ref, vc_ref, kw_ref, vw_ref, oc_ref, ow_ref, sel_ref,
                             *, s_len, past, wb):
    hg = N_HEADS // KV_A
    slope, qpos = _sample_cols(slopes_ref, s_len, past)
    qx = qx_ref[0]
    ncp = kc_ref.shape[1]
    n_cmp = past // CMP_STRIDE - CMP_LEN // CMP_STRIDE + 1
    n_sel = -(-(past + s_len) // SEL_BLOCK)
    nbl = sel_ref.shape[2]

    cidx = lax.broadcasted_iota(I32, (1, ncp), 1)
    dist = qpos - (cidx * CMP_STRIDE + CMP_LEN - 1)
    s = _mm_nt(qx, kc_ref[0])
    s = jnp.where((dist >= 0) & (cidx < n_cmp), s - slope * dist.astype(F32), -jnp.inf)
    p = _softmax_rows(s)
    oc_ref[0] = _mm(p.astype(BF16), vc_ref[0])

    imps = []
    for g in range(KV_A):
        base = g * hg * s_len
        imp = p[base:base + s_len]
        for h in range(1, hg):
            imp = imp + p[base + h * s_len:base + (h + 1) * s_len]
        imps.append(imp)
    imp = jnp.concatenate(imps, axis=0)
    crow = lax.broadcasted_iota(I32, (ncp, nbl), 0) * CMP_STRIDE
    scol = lax.broadcasted_iota(I32, (ncp, nbl), 1) * SEL_BLOCK
    overlap = ((crow < scol + SEL_BLOCK) & (crow + CMP_LEN > scol)
               & (crow < n_cmp * CMP_STRIDE)).astype(F32)
    score = jnp.dot(imp, overlap, precision=lax.Precision.HIGHEST, preferred_element_type=F32)
    jidx = lax.broadcasted_iota(I32, (1, nbl), 1)
    tpos = past + (lax.broadcasted_iota(I32, (KV_A * s_len, 1), 0) % s_len)
    qb = tpos // SEL_BLOCK
    forced = (jidx == 0) | (jidx == qb) | (jidx == qb - 1)
    causal = jidx <= qb
    sc = jnp.where(forced, jnp.inf, jnp.where(causal, score, -jnp.inf))
    sel = (_topk_mask(sc, n_sel, SEL_TOPK) & causal).astype(F32)
    sel_ref[0] = jnp.concatenate([sel[g * s_len:(g + 1) * s_len] for g in range(KV_A) for _ in range(hg)], axis=0)

    ow_ref[0] = _window_sample(qx, kw_ref[0], vw_ref[0], qpos, slope, wb, WIN_A, past)


def nsa_sample_small(slopes, qx, kcmp, vcmp, kwt_all, vwt_all, past, wb, nbl):
    n, rows, w = qx.shape
    s_len = rows // N_HEADS
    ncp = kcmp.shape[1]
    lw = kwt_all.shape[2]
    per_seq = lambda shape: pl.BlockSpec((1,) + shape, lambda i, s: (i, 0, 0))
    return pl.pallas_call(
        functools.partial(_nsa_sample_small_kernel, s_len=s_len, past=past, wb=wb),
        grid_spec=pltpu.PrefetchScalarGridSpec(
            num_scalar_prefetch=1, grid=(n,),
            in_specs=[per_seq((rows, w)), per_seq((ncp, w)), per_seq((ncp, w)), per_seq((w, lw)), per_seq((w, lw))],
            out_specs=[per_seq((rows, w)), per_seq((rows, w)), per_seq((rows, nbl))]),
        out_shape=[jax.ShapeDtypeStruct((n, rows, w), F32), jax.ShapeDtypeStruct((n, rows, w), F32),
                   jax.ShapeDtypeStruct((n, rows, nbl), F32)],
        compiler_params=_cparams(("parallel",)), name="nsa_sample_small",
    )(slopes, qx, kcmp, vcmp, kwt_all, vwt_all)


def _swa_sample_kernel(slopes_ref, sinks_ref, qx_ref, k_ref, v_ref, o_ref, *, s_len, past, wb):
    slope, qpos = _sample_cols(slopes_ref, s_len, past)
    sink = _col(sinks_ref, range(N_HEADS), s_len)
    o_ref[0] = _window_sample(qx_ref[0], k_ref[0], v_ref[0], qpos, slope, wb, WIN_B, past, sink)


def swa_sample_attn(slopes, sinks, qx, kt_all, vt_all, past, wb):
    n, rows, w = qx.shape
    s_len = rows // N_HEADS
    lw = kt_all.shape[2]
    per_seq = lambda shape: pl.BlockSpec((1,) + shape, lambda i, s, k: (i, 0, 0))
    return pl.pallas_call(
        functools.partial(_swa_sample_kernel, s_len=s_len, past=past, wb=wb),
        grid_spec=pltpu.PrefetchScalarGridSpec(
            num_scalar_prefetch=2, grid=(n,),
            in_specs=[per_seq((rows, w)), per_seq((w, lw)), per_seq((w, lw))],
            out_specs=per_seq((rows, w))),
        out_shape=jax.ShapeDtypeStruct((n, rows, w), F32),
        compiler_params=_cparams(("parallel",)), name="swa_sample_attn",
    )(slopes, sinks, qx, kt_all, vt_all)


def _paged_attn_kernel(ptab_ref, slopes_ref, qx_ref, *refs, layer, n_pages, bs, tks, s_len, past, from_means):
    if from_means:
        knew_ref, vnew_ref, kpool, vpool, o_ref, kbuf, vbuf, sem, m_ref, l_ref, acc_ref, selb_ref = refs
    else:
        sel_ref, knew_ref, vnew_ref, kpool, vpool, o_ref, kbuf, vbuf, sem, m_ref, l_ref, acc_ref, selb_ref = refs
    slot = _paged_pipeline(ptab_ref, (kpool, vpool), (kbuf, vbuf), sem, layer, 1, n_pages, True)
    slope, qpos = _sample_cols(slopes_ref, s_len, past)
    qx = qx_ref[0]
    nbw = selb_ref.shape[1] - LANES

    m_ref[...] = jnp.full_like(m_ref, M_INIT)
    l_ref[...] = jnp.zeros_like(l_ref)
    acc_ref[...] = jnp.zeros_like(acc_ref)
    selb_ref[:, nbw:] = jnp.zeros((selb_ref.shape[0], LANES), F32)
    if from_means:
        lane = lax.broadcasted_iota(I32, (kbuf.shape[1], nbw), 1)
        km = jnp.zeros((kbuf.shape[1], nbw), F32)
        for b in range(past // bs):
            blk = kbuf[slot, :, b * bs:(b + 1) * bs]
            km = jnp.where(lane == b, jnp.sum(blk, axis=1, keepdims=True) / bs, km)
        kmh = km.astype(BF16)
        kml = (km - kmh.astype(F32)).astype(BF16)
        gate = _mm(qx, kmh) + _mm(qx, kml)
        bidx = lax.broadcasted_iota(I32, (1, km.shape[1]), 1)
        past_blk = bidx < qpos // bs
        sc = jnp.where(past_blk, gate, -jnp.inf)
        selb_ref[:, :nbw] = (_topk_mask(sc, past // bs, MOBA_TOPK) & past_blk).astype(F32)
    else:
        selb_ref[:, :nbw] = sel_ref[0]

    def update(s, vt):
        m_old = m_ref[...]
        m_new = jnp.maximum(m_old, jnp.max(s, axis=1, keepdims=True))
        alpha = jnp.exp(m_old - m_new)
        p = jnp.exp(s - m_new)
        l_ref[...] = alpha * l_ref[...] + jnp.sum(p, axis=1, keepdims=True)
        acc_ref[...] = alpha * acc_ref[...] + _mm_nt(p.astype(BF16), vt)
        m_ref[...] = m_new

    key_in_tile = lax.broadcasted_iota(I32, (1, tks), 1)
    onehot = (lax.broadcasted_iota(I32, (LANES, tks), 1) // bs
              == lax.broadcasted_iota(I32, (LANES, tks), 0)).astype(BF16)
    rise = slope * key_in_tile.astype(F32)
    for t in range(n_pages * PAGE // tks):
        kt = kbuf[slot, :, t * tks:(t + 1) * tks].astype(BF16)
        vt = vbuf[slot, :, t * tks:(t + 1) * tks].astype(BF16)
        blk0 = t * (tks // bs)
        keep = _mm(selb_ref[:, blk0:blk0 + LANES].astype(BF16), onehot) > 0.5
        s = _mm(qx, kt) + rise - slope * (qpos - t * tks).astype(F32)
        update(jnp.where(keep, s, -jnp.inf), vt)

    j = lax.broadcasted_iota(I32, (1, knew_ref.shape[2]), 1)
    dist = qpos - (past + j)
    s = _mm(qx, knew_ref[0].astype(BF16))
    s = jnp.where((dist >= 0) & (j < s_len), s - slope * dist.astype(F32), -jnp.inf)
    update(s, vnew_ref[0].astype(BF16))
    o_ref[0] = acc_ref[...] / jnp.maximum(l_ref[...], TINY)


def paged_attn(ptab, slopes, qx, sel, knew_t, vnew_t, pool_k, pool_v, layer, *, bs, tks, past):
    n, n_pages = ptab.shape
    _, rows, w = qx.shape
    s_len = rows // N_HEADS
    from_means = sel is None
    nbw = LANES if from_means else sel.shape[2]
    per_seq = lambda shape: pl.BlockSpec((1,) + shape, lambda i, c, pt, sl: (i, 0, 0))
    anyspec = pl.BlockSpec(memory_space=pl.ANY)
    sel_spec, sel_arg = ([], []) if from_means else ([per_seq(sel.shape[1:])], [sel])
    return pl.pallas_call(
        functools.partial(_paged_attn_kernel, layer=layer, n_pages=n_pages, bs=bs, tks=tks, s_len=s_len, past=past,
                          from_means=from_means),
        grid_spec=pltpu.PrefetchScalarGridSpec(
            num_scalar_prefetch=2, grid=(n, 1),
            in_specs=[per_seq((rows, w))] + sel_spec + [per_seq(knew_t.shape[1:]),
                      per_seq(vnew_t.shape[1:]), anyspec, anyspec],
            out_specs=per_seq((rows, w)),
            scratch_shapes=[pltpu.VMEM((2, w, n_pages * PAGE), F32), pltpu.VMEM((2, w, n_pages * PAGE), F32),
                            pltpu.SemaphoreType.DMA((2, 2)),
                            pltpu.VMEM((rows, 1), F32), pltpu.VMEM((rows, 1), F32), pltpu.VMEM((rows, w), F32),
                            pltpu.VMEM((rows, nbw + LANES), F32)]),
        out_shape=jax.ShapeDtypeStruct((n, rows, w), F32),
        compiler_params=_cparams(("arbitrary", "arbitrary")), name="paged_attn",
    )(ptab, slopes, qx, *sel_arg, knew_t, vnew_t, pool_k, pool_v)


def _stack_queries(q, n, s_len, groups):
    hg = N_HEADS // groups
    qh = q.reshape(n, s_len, groups, hg, 1, HEAD_DIM).transpose(0, 2, 3, 1, 4, 5)
    eye = jnp.eye(groups, dtype=q.dtype).reshape(1, groups, 1, 1, groups, 1)
    return (qh * eye).reshape(n, N_HEADS * s_len, groups * HEAD_DIM)


def _unstack_outputs(o, n, s_len, groups):
    hg = N_HEADS // groups
    o6 = o.reshape(n, groups, hg, s_len, groups, HEAD_DIM)
    diag = jnp.stack([o6[:, g, :, :, g, :] for g in range(groups)], axis=1)
    return diag.transpose(0, 3, 1, 2, 4).reshape(n * s_len, Q_WIDTH)


def _feature_major(cache):
    l, p, r, g, d = cache.shape
    return cache.transpose(0, 1, 3, 4, 2).reshape(l, p, g * d, r)


def _state(kv_t, branch, groups):
    n, _, t = kv_t.shape
    gw = groups * HEAD_DIM
    return kv_t[:, branch * gw:(branch + 1) * gw].reshape(n, groups, HEAD_DIM, t).transpose(0, 3, 1, 2)


def _window_with_new(buf_t, new_rows, groups):
    n, w, wb = buf_t.shape
    s_len = new_rows.shape[1]
    both = jnp.concatenate([buf_t, new_rows.transpose(0, 2, 1)], axis=2)
    state = both[:, :, s_len:].reshape(n, groups, HEAD_DIM, wb).transpose(0, 3, 1, 2)
    lw = -(-(wb + s_len) // LANES) * LANES
    return jnp.pad(both, ((0, 0), (0, 0), (0, lw - wb - s_len))), state


def _new_rows_t(rows):
    return jnp.pad(rows.transpose(0, 2, 1), ((0, 0), (0, 0), (0, LANES - rows.shape[1])))


def _nsa_layer(xp, xs, g_pre, w_in, cw, layer, caches, ptab, slopes):
    cmp_k, cmp_v, sel_k, sel_v, win_k, win_v = caches
    n_p, t, _ = xp.shape
    n_s = ptab.shape[0]
    s_len = xs.shape[0] // n_s
    past = ptab.shape[1] * PAGE
    gw = KV_A * HEAD_DIM

    q_t, kv_t, kv_tb, k_rows, cmp_rows, gates_t = proj_prompt(xp, g_pre, w_in, (4, 5, 8, 9), True)
    kcmp, vcmp = compress_prompt(cmp_rows, cw)
    ncp = max(kcmp.shape[1], LANES)
    pad = ((0, 0), (0, ncp - kcmp.shape[1]), (0, 0))
    kc = jnp.pad(kcmp, pad).reshape(n_p, ncp, KV_A, HEAD_DIM).transpose(0, 2, 1, 3)
    vc_t = jnp.pad(vcmp, pad).reshape(n_p, ncp, KV_A, HEAD_DIM).transpose(0, 2, 3, 1)
    o_t = nsa_prompt_attn(slopes, q_t, k_rows, kv_tb, kc, vc_t, gates_t)
    wb = min(WIN_A, t)
    st_p = tuple(_state(kv_t, i, KV_A) for i in range(4)) + tuple(_state(kv_t[:, :, t - wb:], i, KV_A) for i in (4, 5))

    q, kv, gates = proj_sample(xs, g_pre, w_in, True)
    kv4 = kv.reshape(n_s, s_len, 6, gw)
    kwt_all, st_wk = _window_with_new(_feature_major(win_k)[layer], kv4[:, :, 4], KV_A)
    vwt_all, st_wv = _window_with_new(_feature_major(win_v)[layer], kv4[:, :, 5], KV_A)
    kcmp, vcmp = compress_sample(ptab, _feature_major(cmp_k), _feature_major(cmp_v), layer, cw)
    qx = _stack_queries(q, n_s, s_len, KV_A)
    nbl = -(-(-(-(past + s_len) // SEL_BLOCK)) // LANES) * LANES
    o_c, o_w, sel = nsa_sample_small(slopes, qx, kcmp, vcmp, kwt_all, vwt_all, past, win_k.shape[2], nbl)
    o_s = paged_attn(ptab, slopes, qx, sel, _new_rows_t(kv4[:, :, 2]), _new_rows_t(kv4[:, :, 3]),
                     _feature_major(sel_k), _feature_major(sel_v), layer,
                     bs=SEL_BLOCK, tks=min(PAGED_TK, past), past=past)
    o_s_list = [_unstack_outputs(o, n_s, s_len, KV_A) for o in (o_c, o_s, o_w)]
    gate_list = [jnp.repeat(gates[:, br * N_HEADS:(br + 1) * N_HEADS], HEAD_DIM, axis=1) for br in range(3)]
    kv6 = kv.reshape(n_s, s_len, 6, KV_A, HEAD_DIM)
    st_s = tuple(kv6[:, :, i] for i in range(4)) + (st_wk, st_wv)
    return o_t, o_s_list, gate_list, st_p, st_s


def _swa_layer(xp, xs, g_pre, w_in, sinks, layer, caches, past, slopes, n_s):
    buf_k, buf_v = caches
    n_p, t, _ = xp.shape
    s_len = xs.shape[0] // n_s
    gw = KV_B * HEAD_DIM

    q_t, kv_t, kv_tb, k_rows = proj_prompt(xp, g_pre, w_in, (0, 1), False)
    o_t = swa_prompt_attn(slopes, sinks, q_t, k_rows, kv_tb)
    wb = min(WIN_B, t)
    st_p = tuple(_state(kv_t[:, :, t - wb:], i, KV_B) for i in (0, 1))

    q, kv = proj_sample(xs, g_pre, w_in, False)
    kv4 = kv.reshape(n_s, s_len, 2, gw)
    kt_all, st_k = _window_with_new(_feature_major(buf_k)[layer], kv4[:, :, 0], KV_B)
    vt_all, st_v = _window_with_new(_feature_major(buf_v)[layer], kv4[:, :, 1], KV_B)
    o = swa_sample_attn(slopes, sinks, _stack_queries(q, n_s, s_len, KV_B), kt_all, vt_all, past, buf_k.shape[2])
    return o_t, [_unstack_outputs(o, n_s, s_len, KV_B)], st_p, (st_k, st_v)


def _split_hi_lo(x):
    hi = x.astype(BF16)
    return hi, (x - hi.astype(F32)).astype(BF16)


def _moba_layer(xp, xs, g_pre, w_in, layer, caches, ptab, slopes):
    pool_k, pool_v = caches
    n_p, t, _ = xp.shape
    n_s = ptab.shape[0]
    s_len = xs.shape[0] // n_s
    gw = KV_C * HEAD_DIM
    past = ptab.shape[1] * PAGE

    q_t, kv_t, kv_tb, k_rows = proj_prompt(xp, g_pre, w_in, tuple(range(KV_C)), False)
    kmean = block_means_prompt(kv_t).reshape(n_p, KV_C, HEAD_DIM, LANES).transpose(0, 1, 3, 2)
    kmh, kml = _split_hi_lo(kmean)
    o_t = moba_prompt_attn(slopes, q_t, k_rows, kv_tb, kmh, kml)
    st_p = (_state(kv_t, 0, KV_C), _state(kv_t, 1, KV_C))

    q, kv = proj_sample(xs, g_pre, w_in, False)
    kv4 = kv.reshape(n_s, s_len, 2, gw)
    pk_t, pv_t = _feature_major(pool_k), _feature_major(pool_v)
    o = paged_attn(ptab, slopes, _stack_queries(q, n_s, s_len, KV_C), None,
                   _new_rows_t(kv4[:, :, 0]), _new_rows_t(kv4[:, :, 1]), pk_t, pv_t, layer,
                   bs=MOBA_BLOCK, tks=min(PAGED_TK, past), past=past)
    kv5 = kv.reshape(n_s, s_len, 2, KV_C, HEAD_DIM)
    return o_t, [_unstack_outputs(o, n_s, s_len, KV_C)], st_p, (kv5[:, :, 0], kv5[:, :, 1])


def kernel(x_prompt, x_sample, cache_nsa_cmp_k, cache_nsa_cmp_v, cache_nsa_sel_k, cache_nsa_sel_v, cache_nsa_win_k, cache_nsa_win_v, cache_swa_k, cache_swa_v, cache_moba_k, cache_moba_v, page_table, norm_pre, norm_post, ffn_w_gate, ffn_w_up, ffn_w_down, nsa_w_in, nsa_cmp_pos_k, nsa_cmp_w1_k, nsa_cmp_w2_k, nsa_cmp_pos_v, nsa_cmp_w1_v, nsa_cmp_w2_v, nsa_w_out, swa_w_in, swa_sinks, swa_w_out, moba_w_in, moba_w_out):
    n_p, t, d = x_prompt.shape
    n_s, s_len, _ = x_sample.shape
    depth = norm_pre.shape[0]
    slopes = jnp.exp2(-8.0 * jnp.arange(1, N_HEADS + 1, dtype=F32) / N_HEADS)
    past = page_table.shape[1] * PAGE
    xp = x_prompt.reshape(n_p * t, d)
    xs = x_sample.reshape(n_s * s_len, d)
    nsa_p, nsa_s, swa_p, swa_s, moba_p, moba_s = [], [], [], [], [], []

    def ffn_both(xp, xs, i, which, slot):
        wg, wu, wd = (w[i, which].astype(BF16) for w in (ffn_w_gate, ffn_w_up, ffn_w_down))
        return (ffn(xp, norm_pre[i, slot], norm_post[i, slot], wg, wu, wd),
                ffn(xs, norm_pre[i, slot], norm_post[i, slot], wg, wu, wd))

    for i in range(depth):
        kind, j = i % 3, i // 3
        xp, xs = ffn_both(xp, xs, i, 0, 0)
        xp3 = xp.reshape(n_p, t, d)
        gates_s = None
        if kind == 0:
            cw = {"k": _compress_weights(nsa_cmp_pos_k[j], nsa_cmp_w1_k[j], nsa_cmp_w2_k[j]),
                  "v": _compress_weights(nsa_cmp_pos_v[j], nsa_cmp_w1_v[j], nsa_cmp_w2_v[j])}
            o_t, o_s, gates_s, st_p, st_s = _nsa_layer(
                xp3, xs, norm_pre[i, 1], nsa_w_in[j], cw, j,
                (cache_nsa_cmp_k, cache_nsa_cmp_v, cache_nsa_sel_k, cache_nsa_sel_v, cache_nsa_win_k, cache_nsa_win_v),
                page_table, slopes)
            nsa_p.append(st_p)
            nsa_s.append(st_s)
            w_out = nsa_w_out[j]
        elif kind == 1:
            o_t, o_s, st_p, st_s = _swa_layer(xp3, xs, norm_pre[i, 1], swa_w_in[j], swa_sinks[j], j,
                                              (cache_swa_k, cache_swa_v), past, slopes, n_s)
            swa_p.append(st_p)
            swa_s.append(st_s)
            w_out = swa_w_out[j]
        else:
            o_t, o_s, st_p, st_s = _moba_layer(xp3, xs, norm_pre[i, 1], moba_w_in[j], j,
                                               (cache_moba_k, cache_moba_v), page_table, slopes)
            moba_p.append(st_p)
            moba_s.append(st_s)
            w_out = moba_w_out[j]
        w_out = w_out.astype(BF16)
        xp = out_proj_prompt(o_t, w_out, norm_post[i, 1], xp3).reshape(n_p * t, d)
        xs = out_proj(o_s, w_out, norm_post[i, 1], xs, gates_s)
        xp, xs = ffn_both(xp, xs, i, 1, 2)

    stack = lambda states: [jnp.stack(a) for a in zip(*states)]
    return (xp.reshape(n_p, t, d), xs.reshape(n_s, s_len, d),
            *stack(nsa_p), *stack(swa_p), *stack(moba_p), *stack(nsa_s), *stack(swa_s), *stack(moba_s))
```

```python
import functools

import jax
import jax.numpy as jnp
from jax import lax
from jax.experimental import pallas as pl
from jax.experimental.pallas import tpu as pltpu

F32 = jnp.float32
BF16 = jnp.bfloat16
I32 = jnp.int32

D_MODEL = 1024
N_HEADS = 16
HEAD_DIM = 64
Q_WIDTH = N_HEADS * HEAD_DIM
RMS_EPS = 1e-6
ATTN_SCALE = HEAD_DIM ** -0.5
KV_A, KV_B, KV_C = 2, 2, 4
CMP_LEN, CMP_STRIDE, CMP_HID = 32, 16, 256
SEL_BLOCK, SEL_TOPK = 64, 16
WIN_A, WIN_B = 512, 128
MOBA_BLOCK, MOBA_TOPK = 256, 3
PAGE = 128

LANES = 128
TINY = float(jnp.finfo(jnp.float32).tiny)
M_INIT = -1e30
VMEM_LIMIT = 56 * 1024 * 1024

TQ = 128
FFN_TM, FFN_TF = 1024, 256
PROJ_TM = 512
PAGED_TK = 2048
FLASH_GROUP = 4
MOBA_GROUP = 2
assert WIN_A % TQ == 0 and WIN_B % TQ == 0 and MOBA_BLOCK % TQ == 0 and PAGED_TK % MOBA_BLOCK == 0


def _cparams(sem):
    return pltpu.CompilerParams(dimension_semantics=sem, vmem_limit_bytes=VMEM_LIMIT)


def _mm_nt(a, b):
    return lax.dot_general(a, b, (((1,), (1,)), ((), ())), preferred_element_type=F32)


def _mm(a, b):
    return jnp.dot(a, b, preferred_element_type=F32)


def _rms(x, g):
    return x * lax.rsqrt(jnp.mean(x * x, axis=-1, keepdims=True) + RMS_EPS) * g


def _ffn_kernel(x_ref, gpre_ref, gpost_ref, wg_ref, wu_ref, wd_ref, o_ref, xn_ref, acc_ref):
    f = pl.program_id(1)

    @pl.when(f == 0)
    def _():
        xn_ref[...] = _rms(x_ref[...], gpre_ref[...]).astype(BF16)
        acc_ref[...] = jnp.zeros_like(acc_ref)

    xn = xn_ref[...]
    g = _mm(xn, wg_ref[...])
    u = _mm(xn, wu_ref[...])
    h = g * jax.nn.sigmoid(g) * u
    acc_ref[...] += _mm(h.astype(BF16), wd_ref[...])

    @pl.when(f == pl.num_programs(1) - 1)
    def _():
        o_ref[...] = x_ref[...] + 0.5 * _rms(acc_ref[...], gpost_ref[...])


def ffn(x, g_pre, g_post, wg, wu, wd):
    m, d = x.shape
    dff = wg.shape[1]
    tm = min(FFN_TM, m)
    return pl.pallas_call(
        _ffn_kernel,
        grid=(m // tm, dff // FFN_TF),
        in_specs=[
            pl.BlockSpec((tm, d), lambda i, f: (i, 0)),
            pl.BlockSpec((1, d), lambda i, f: (0, 0)),
            pl.BlockSpec((1, d), lambda i, f: (0, 0)),
            pl.BlockSpec((d, FFN_TF), lambda i, f: (0, f)),
            pl.BlockSpec((d, FFN_TF), lambda i, f: (0, f)),
            pl.BlockSpec((FFN_TF, d), lambda i, f: (f, 0)),
        ],
        out_specs=pl.BlockSpec((tm, d), lambda i, f: (i, 0)),
        out_shape=jax.ShapeDtypeStruct((m, d), F32),
        scratch_shapes=[pltpu.VMEM((tm, d), BF16), pltpu.VMEM((tm, d), F32)],
        compiler_params=_cparams(("parallel", "arbitrary")),
        name="ffn",
    )(x, g_pre.reshape(1, d), g_post.reshape(1, d), wg, wu, wd)


def _proj_prompt_kernel(x_ref, g_ref, wqt_ref, wkvt_ref, wk_ref, *rest, nsa):
    if nsa:
        wc_ref, wgt_ref, qt_ref, kvt_ref, kvtb_ref, k_ref, c_ref, gt_ref = rest
    else:
        qt_ref, kvt_ref, kvtb_ref, k_ref = rest
    xn = _rms(x_ref[0], g_ref[...]).astype(BF16)
    qt_ref[0] = (_mm_nt(wqt_ref[...], xn) * ATTN_SCALE).astype(BF16)
    kvt = _mm_nt(wkvt_ref[...], xn)
    kvt_ref[0] = kvt
    kvtb_ref[0] = kvt.astype(BF16)
    for c in range(wk_ref.shape[0]):
        k_ref[0, c] = _mm(xn, wk_ref[c]).astype(BF16)
    if nsa:
        c_ref[0] = _mm(xn, wc_ref[...])
        gt_ref[0] = jax.nn.sigmoid(_mm_nt(wgt_ref[...], xn))


def proj_prompt(x, g, w_in, k_cols, nsa):
    n, t, d = x.shape
    kvw = w_in.shape[1] - Q_WIDTH - (3 * N_HEADS if nsa else 0)
    wqt = w_in[:, :Q_WIDTH].T.astype(BF16)
    wkv = w_in[:, Q_WIDTH:Q_WIDTH + kvw]
    wkvt = wkv.T.astype(BF16)
    wk = jnp.stack([wkv[:, c * HEAD_DIM:(c + 1) * HEAD_DIM] for c in k_cols]).astype(BF16)
    ck = len(k_cols)
    fixed2 = lambda i, j: (0, 0)
    fixed3 = lambda i, j: (0, 0, 0)
    in_specs = [pl.BlockSpec((1, TQ, d), lambda i, j: (i, j, 0)), pl.BlockSpec((1, d), fixed2),
                pl.BlockSpec((Q_WIDTH, d), fixed2), pl.BlockSpec((kvw, d), fixed2),
                pl.BlockSpec((ck, d, HEAD_DIM), fixed3)]
    out_specs = [pl.BlockSpec((1, Q_WIDTH, TQ), lambda i, j: (i, 0, j)),
                 pl.BlockSpec((1, kvw, TQ), lambda i, j: (i, 0, j)),
                 pl.BlockSpec((1, kvw, TQ), lambda i, j: (i, 0, j)),
                 pl.BlockSpec((1, ck, TQ, HEAD_DIM), lambda i, j: (i, 0, j, 0))]
    out_shape = [jax.ShapeDtypeStruct((n, Q_WIDTH, t), BF16),
                 jax.ShapeDtypeStruct((n, kvw, t), F32), jax.ShapeDtypeStruct((n, kvw, t), BF16),
                 jax.ShapeDtypeStruct((n, ck, t, HEAD_DIM), BF16)]
    args = [x, g.reshape(1, d), wqt, wkvt, wk]
    if nsa:
        wgt = jnp.pad(w_in[:, Q_WIDTH + kvw:], ((0, 0), (0, LANES - 3 * N_HEADS))).T.astype(BF16)
        in_specs += [pl.BlockSpec((d, 2 * LANES), fixed2), pl.BlockSpec((LANES, d), fixed2)]
        out_specs += [pl.BlockSpec((1, TQ, 2 * LANES), lambda i, j: (i, j, 0)),
                      pl.BlockSpec((1, LANES, TQ), lambda i, j: (i, 0, j))]
        out_shape += [jax.ShapeDtypeStruct((n, t, 2 * LANES), F32), jax.ShapeDtypeStruct((n, LANES, t), F32)]
        args += [wkv[:, :2 * LANES].astype(BF16), wgt]
    return pl.pallas_call(
        functools.partial(_proj_prompt_kernel, nsa=nsa),
        grid=(n, t // TQ), in_specs=in_specs, out_specs=out_specs, out_shape=out_shape,
        compiler_params=_cparams(("parallel", "parallel")), name="proj_prompt",
    )(*args)


def _proj_kernel(x_ref, g_ref, wq_ref, wkv_ref, *rest, has_gates):
    if has_gates:
        wg_ref, q_ref, kv_ref, gate_ref = rest
    else:
        q_ref, kv_ref = rest
    xn = _rms(x_ref[...], g_ref[...]).astype(BF16)
    q_ref[...] = (_mm(xn, wq_ref[...]) * ATTN_SCALE).astype(BF16)
    kv_ref[...] = _mm(xn, wkv_ref[...])
    if has_gates:
        gate_ref[...] = jax.nn.sigmoid(_mm(xn, wg_ref[...]))


def proj_sample(x, g, w_in, nsa):
    m, d = x.shape
    kvw = w_in.shape[1] - Q_WIDTH - (3 * N_HEADS if nsa else 0)
    tm = min(PROJ_TM, m)
    row = lambda i: (i, 0)
    fixed = lambda i: (0, 0)
    in_specs = [pl.BlockSpec((tm, d), row), pl.BlockSpec((1, d), fixed),
                pl.BlockSpec((d, Q_WIDTH), fixed), pl.BlockSpec((d, kvw), fixed)]
    out_specs = [pl.BlockSpec((tm, Q_WIDTH), row), pl.BlockSpec((tm, kvw), row)]
    out_shape = [jax.ShapeDtypeStruct((m, Q_WIDTH), BF16), jax.ShapeDtypeStruct((m, kvw), F32)]
    args = [x, g.reshape(1, d), w_in[:, :Q_WIDTH].astype(BF16), w_in[:, Q_WIDTH:Q_WIDTH + kvw].astype(BF16)]
    if nsa:
        in_specs.append(pl.BlockSpec((d, LANES), fixed))
        out_specs.append(pl.BlockSpec((tm, LANES), row))
        out_shape.append(jax.ShapeDtypeStruct((m, LANES), F32))
        args.append(jnp.pad(w_in[:, Q_WIDTH + kvw:], ((0, 0), (0, LANES - 3 * N_HEADS))).astype(BF16))
    return pl.pallas_call(
        functools.partial(_proj_kernel, has_gates=nsa),
        grid=(m // tm,), in_specs=in_specs, out_specs=out_specs, out_shape=out_shape,
        compiler_params=_cparams(("parallel",)), name="proj_sample",
    )(*args)


def _out_prompt_kernel(ot_ref, w_ref, gp_ref, x_ref, y_ref):
    mix = lax.dot_general(ot_ref[0], w_ref[...], (((0,), (0,)), ((), ())), preferred_element_type=F32)
    y_ref[0] = x_ref[0] + _rms(mix, gp_ref[...])


def out_proj_prompt(o_t, w_out, g_post, x):
    n, t, d = x.shape
    return pl.pallas_call(
        _out_prompt_kernel,
        grid=(n, t // TQ),
        in_specs=[pl.BlockSpec((1, Q_WIDTH, TQ), lambda i, j: (i, 0, j)),
                  pl.BlockSpec((Q_WIDTH, d), lambda i, j: (0, 0)),
                  pl.BlockSpec((1, d), lambda i, j: (0, 0)),
                  pl.BlockSpec((1, TQ, d), lambda i, j: (i, j, 0))],
        out_specs=pl.BlockSpec((1, TQ, d), lambda i, j: (i, j, 0)),
        out_shape=jax.ShapeDtypeStruct((n, t, d), F32),
        compiler_params=_cparams(("parallel", "parallel")), name="out_proj_prompt",
    )(o_t, w_out, g_post.reshape(1, d), x)


def _out_kernel(*refs, n_o, gated):
    o_refs = refs[:n_o]
    pos = n_o
    if gated:
        g_refs = refs[pos:pos + n_o]
        pos += n_o
    w_ref, gp_ref, x_ref, y_ref = refs[pos:pos + 4]
    o = None
    for i in range(n_o):
        t = o_refs[i][...]
        if gated:
            t = g_refs[i][...] * t
        o = t if o is None else o + t
    y_ref[...] = x_ref[...] + _rms(_mm(o.astype(BF16), w_ref[...]), gp_ref[...])


def out_proj(o_list, w_out, g_post, x, gate_list=None):
    m, d = x.shape
    tm = min(PROJ_TM, m)
    n_o = len(o_list)
    gated = gate_list is not None
    row = lambda i: (i, 0)
    fixed = lambda i: (0, 0)
    in_specs = [pl.BlockSpec((tm, Q_WIDTH), row)] * (n_o * (2 if gated else 1))
    in_specs += [pl.BlockSpec((Q_WIDTH, d), fixed), pl.BlockSpec((1, d), fixed), pl.BlockSpec((tm, d), row)]
    args = list(o_list) + (list(gate_list) if gated else []) + [w_out, g_post.reshape(1, d), x]
    return pl.pallas_call(
        functools.partial(_out_kernel, n_o=n_o, gated=gated),
        grid=(m // tm,), in_specs=in_specs, out_specs=pl.BlockSpec((tm, d), row),
        out_shape=jax.ShapeDtypeStruct((m, d), F32),
        compiler_params=_cparams(("parallel",)), name="out_proj",
    )(*args)


def _col(ref, heads, rows):
    return jnp.concatenate([jnp.full((rows, 1), ref[h], F32) for h in heads], axis=0)


def _row(ref, heads, cols):
    return jnp.concatenate([jnp.full((1, cols), ref[h], F32) for h in heads], axis=1)


def _group_queries(qt_ref, g, hg):
    base = g * hg * HEAD_DIM
    return jnp.concatenate([qt_ref[0, base + h * HEAD_DIM:base + (h + 1) * HEAD_DIM, :] for h in range(hg)], axis=1)


def _store_group(ot_ref, g, hg, o):
    base = g * hg * HEAD_DIM
    for h in range(hg):
        ot_ref[0, base + h * HEAD_DIM:base + (h + 1) * HEAD_DIM, :] = o[:, h * TQ:(h + 1) * TQ].astype(BF16)


def _expand_blocks(sel, key0, tk, bs):
    nb = sel.shape[1]
    blk = (key0 + lax.broadcasted_iota(I32, (nb, tk), 1)) // bs
    onehot = (blk == lax.broadcasted_iota(I32, (nb, tk), 0)).astype(BF16)
    return _mm(sel, onehot)


def _expand_blocks_t(sel_t, key0, tk, bs):
    nb = sel_t.shape[0]
    blk = (key0 + lax.broadcasted_iota(I32, (tk, nb), 0)) // bs
    onehot = (blk == lax.broadcasted_iota(I32, (tk, nb), 1)).astype(BF16)
    return _mm(onehot, sel_t)


def _topk_mask(sc, nb, topk):
    jidx = lax.broadcasted_iota(I32, sc.shape, 1)
    rank = jnp.zeros(sc.shape, I32)
    for jp in range(nb):
        col = sc[:, jp:jp + 1]
        ahead = (col > sc) | ((col == sc) & (jidx > jp))
        rank = rank + ahead.astype(I32)
    return rank < topk


def _topk_mask_t(sc, nb, topk):
    jidx = lax.broadcasted_iota(I32, sc.shape, 0)
    rank = jnp.zeros(sc.shape, I32)
    for jp in range(nb):
        row = sc[jp:jp + 1, :]
        ahead = (row > sc) | ((row == sc) & (jidx > jp))
        rank = rank + ahead.astype(I32)
    return rank < topk


def _alibi_rise(slopes, groups, n):
    hg = N_HEADS // groups
    lane_slope = jnp.repeat(slopes.reshape(groups, hg), TQ, axis=1)
    return lane_slope[:, None, :] * jnp.arange(n, dtype=F32)[None, :, None]


def _flash_t(qt, k_at, vt_at, segments, tk, qpos, slope, rise_at, m0, l0, m_ref, l_ref, acc_ref):
    m_ref[...] = m0
    l_ref[...] = l0
    acc_ref[...] = jnp.zeros_like(acc_ref)
    cols = qt.shape[1]
    def body(i, carry, first, n, mask_fn):
        key0 = (first + i * (n // tk)) * tk
        ahead = qpos - key0
        s = _mm(k_at(key0, n), qt) + rise_at(n) - slope * ahead.astype(F32)
        if mask_fn is not None:
            s = jnp.where(mask_fn(key0, n, lambda: ahead - lax.broadcasted_iota(I32, (n, cols), 0)), s, -jnp.inf)
        m_old = m_ref[...]
        m_new = jnp.maximum(m_old, jnp.max(s, axis=0, keepdims=True))
        alpha = jnp.exp(m_old - m_new)
        p = jnp.exp(s - m_new)
        l_ref[...] = alpha * l_ref[...] + jnp.sum(p, axis=0, keepdims=True)
        acc_ref[...] = alpha * acc_ref[...] + _mm(vt_at(key0, n), p.astype(BF16))
        m_ref[...] = m_new
        return carry

    for lo, hi, mask_fn, group in segments:
        if group > 1:
            steps = (hi - lo) // group
            lax.fori_loop(0, steps, functools.partial(body, first=lo, n=group * tk, mask_fn=mask_fn), 0)
            lo = lo + steps * group
        lax.fori_loop(0, hi - lo, functools.partial(body, first=lo, n=tk, mask_fn=mask_fn), 0)
    return acc_ref[...] / jnp.maximum(l_ref[...], TINY)


def _softmax_rows(s, sink=None):
    m = jnp.max(s, axis=1, keepdims=True)
    if sink is not None:
        m = jnp.maximum(m, sink)
    m = jnp.where(m > -jnp.inf, m, 0.0)
    e = jnp.exp(s - m)
    den = jnp.sum(e, axis=1, keepdims=True)
    if sink is not None:
        den = den + jnp.exp(sink - m)
    return e / jnp.maximum(den, TINY)


def _softmax_cols(s):
    m = jnp.max(s, axis=0, keepdims=True)
    m = jnp.where(m > -jnp.inf, m, 0.0)
    e = jnp.exp(s - m)
    return e / jnp.maximum(jnp.sum(e, axis=0, keepdims=True), TINY)


def _compress_rows(load_rows, nch, wbig_ref, pos_ref, w2_ref):
    chunks = jnp.concatenate([load_rows(l).astype(BF16) for l in range(CMP_STRIDE)], axis=1)
    acc = _mm(chunks, wbig_ref[...])
    first = acc[:, :2 * CMP_HID]
    second = pltpu.roll(acc[:, 2 * CMP_HID:], nch - 1, 0)
    hid = first + second + pos_ref[0:1, :]
    return _mm(jax.nn.gelu(hid).astype(BF16), w2_ref[...])


def _compress_prompt_kernel(k_ref, v_ref, wk_ref, pk_ref, w2k_ref, wv_ref, pv_ref, w2v_ref, ok_ref, ov_ref, *, nch):
    for x_ref, w, p, w2, o_ref in ((k_ref, wk_ref, pk_ref, w2k_ref, ok_ref), (v_ref, wv_ref, pv_ref, w2v_ref, ov_ref)):
        load = lambda l, x_ref=x_ref: x_ref[0, pl.ds(l, nch, stride=CMP_STRIDE), :]
        o_ref[0] = _compress_rows(load, nch, w, p, w2).astype(BF16)


def compress_prompt(rows, cw):
    n, t, _ = rows.shape
    nch = t // CMP_STRIDE
    wspec = [pl.BlockSpec((CMP_STRIDE * LANES, 4 * CMP_HID), lambda i: (0, 0)),
             pl.BlockSpec((8, 2 * CMP_HID), lambda i: (0, 0)),
             pl.BlockSpec((2 * CMP_HID, LANES), lambda i: (0, 0))]
    out = jax.ShapeDtypeStruct((n, nch, LANES), BF16)
    return pl.pallas_call(
        functools.partial(_compress_prompt_kernel, nch=nch),
        grid=(n,),
        in_specs=[pl.BlockSpec((1, t, LANES), lambda i: (i, 0, 0)),
                  pl.BlockSpec((1, t, LANES), lambda i: (i, 0, 1))] + wspec + wspec,
        out_specs=[pl.BlockSpec((1, nch, LANES), lambda i: (i, 0, 0))] * 2,
        out_shape=[out, out],
        compiler_params=_cparams(("parallel",)), name="compress_prompt",
    )(rows, rows, *cw["k"], *cw["v"])


def _pos_term_kernel(p_ref, w_ref, o_ref):
    o_ref[...] = jnp.dot(p_ref[...], w_ref[...], precision=lax.Precision.HIGHEST, preferred_element_type=F32)


def _compress_weights(pos, w1, w2):
    eye = jnp.eye(KV_A, dtype=F32)
    w1r = w1.reshape(CMP_LEN // CMP_STRIDE, CMP_STRIDE, HEAD_DIM, CMP_HID)
    wbig = jnp.einsum("jldh,ab->lbdjah", w1r, eye).reshape(CMP_STRIDE * LANES, 4 * CMP_HID).astype(BF16)
    w2big = jnp.einsum("hd,ab->ahbd", w2, eye).reshape(KV_A * CMP_HID, LANES).astype(BF16)
    pos8 = jnp.broadcast_to(pos.reshape(1, CMP_LEN * HEAD_DIM), (8, CMP_LEN * HEAD_DIM))
    pterm = pl.pallas_call(
        _pos_term_kernel, out_shape=jax.ShapeDtypeStruct((8, CMP_HID), F32), name="cmp_pos_term",
    )(pos8, w1)
    return wbig, jnp.concatenate([pterm, pterm], axis=1), w2big


def _page_copies(ptab_ref, pools, bufs, sem, layer, n, c, slot, pps, feature_major):
    copies = []
    for i in range(pps):
        page = ptab_ref[n, c * pps + i]
        for j, (pool, buf) in enumerate(zip(pools, bufs)):
            dst = buf.at[slot, :, pl.ds(i * PAGE, PAGE)] if feature_major else buf.at[slot, pl.ds(i * PAGE, PAGE)]
            copies.append(pltpu.make_async_copy(pool.at[layer, page], dst, sem.at[j, slot]))
    return copies


def _paged_pipeline(ptab_ref, pools, bufs, sem, layer, nc, pps, feature_major):
    n, c = pl.program_id(0), pl.program_id(1)
    step = n * nc + c
    slot = step % 2
    copies = functools.partial(_page_copies, ptab_ref, pools, bufs, sem, layer, pps=pps, feature_major=feature_major)

    @pl.when(step == 0)
    def _():
        for cp in copies(n, c, slot):
            cp.start()

    @pl.when(step + 1 < pl.num_programs(0) * nc)
    def _():
        nxt = step + 1
        for cp in copies(nxt // nc, nxt % nc, 1 - slot):
            cp.start()

    for cp in copies(n, c, slot):
        cp.wait()
    return slot


def _compress_sample_kernel(ptab_ref, kpool, vpool, wk_ref, pk_ref, w2k_ref, wv_ref, pv_ref, w2v_ref,
                            ok_ref, ov_ref, kbuf, vbuf, sem, rows_ref, *, layer, n_pages):
    slot = _paged_pipeline(ptab_ref, (kpool, vpool), (kbuf, vbuf), sem, layer, 1, n_pages, True)
    nch = n_pages * PAGE // CMP_STRIDE
    for buf, w, p, w2, o_ref in ((kbuf, wk_ref, pk_ref, w2k_ref, ok_ref), (vbuf, wv_ref, pv_ref, w2v_ref, ov_ref)):
        for i in range(n_pages):
            rows_ref[i * PAGE:(i + 1) * PAGE, :] = buf[slot, :, i * PAGE:(i + 1) * PAGE].T
        load = lambda l: rows_ref[pl.ds(l, nch, stride=CMP_STRIDE), :]
        o_ref[0] = _compress_rows(load, nch, w, p, w2).astype(BF16)


def compress_sample(ptab, pool_k, pool_v, layer, cw):
    n, n_pages = ptab.shape
    rows = n_pages * PAGE
    nch = rows // CMP_STRIDE
    wspec = [pl.BlockSpec((CMP_STRIDE * LANES, 4 * CMP_HID), lambda i, c, pt: (0, 0)),
             pl.BlockSpec((8, 2 * CMP_HID), lambda i, c, pt: (0, 0)),
             pl.BlockSpec((2 * CMP_HID, LANES), lambda i, c, pt: (0, 0))]
    anyspec = pl.BlockSpec(memory_space=pl.ANY)
    out = jax.ShapeDtypeStruct((n, nch, LANES), BF16)
    return pl.pallas_call(
        functools.partial(_compress_sample_kernel, layer=layer, n_pages=n_pages),
        grid_spec=pltpu.PrefetchScalarGridSpec(
            num_scalar_prefetch=1, grid=(n, 1),
            in_specs=[anyspec, anyspec] + wspec + wspec,
            out_specs=[pl.BlockSpec((1, nch, LANES), lambda i, c, pt: (i, 0, 0))] * 2,
            scratch_shapes=[pltpu.VMEM((2, LANES, rows), F32), pltpu.VMEM((2, LANES, rows), F32),
                            pltpu.SemaphoreType.DMA((2, 2)), pltpu.VMEM((rows, LANES), F32)]),
        out_shape=[out, out],
        compiler_params=_cparams(("arbitrary", "arbitrary")), name="compress_sample",
    )(ptab, pool_k, pool_v, *cw["k"], *cw["v"])


def _nsa_prompt_kernel(slopes_ref, qt_ref, k_ref, kvt_ref, kc_ref, vct_ref, gt_ref, rise_ref, ot_ref,
                       m_ref, l_ref, acc_ref, *, t_len):
    qi = pl.program_id(1)
    q0 = qi * TQ
    hg = N_HEADS // KV_A
    cols = hg * TQ
    qpos = q0 + (lax.broadcasted_iota(I32, (1, cols), 1) & (TQ - 1))
    qpos_t = q0 + lax.broadcasted_iota(I32, (1, TQ), 1)
    ncp = kc_ref.shape[2]
    n_cmp = t_len // CMP_STRIDE - CMP_LEN // CMP_STRIDE + 1
    n_sel = t_len // SEL_BLOCK
    m0 = jnp.full((1, cols), M_INIT, F32)
    l0 = jnp.zeros((1, cols), F32)
    gw = KV_A * HEAD_DIM

    for g in range(KV_A):
        heads = range(g * hg, (g + 1) * hg)
        qt = _group_queries(qt_ref, g, hg)
        slope = _row(slopes_ref, heads, TQ)

        cidx = lax.broadcasted_iota(I32, (ncp, cols), 0)
        dist = qpos - (cidx * CMP_STRIDE + CMP_LEN - 1)
        s = _mm(kc_ref[0, g], qt)
        s = jnp.where((dist >= 0) & (cidx < n_cmp), s - slope * dist.astype(F32), -jnp.inf)
        p = _softmax_cols(s)
        o_c = _mm(vct_ref[0, g], p.astype(BF16))
        imp = p[:, 0:TQ]
        for h in range(1, hg):
            imp = imp + p[:, h * TQ:(h + 1) * TQ]

        srow = lax.broadcasted_iota(I32, (LANES, ncp), 0) * SEL_BLOCK
        ccol = lax.broadcasted_iota(I32, (LANES, ncp), 1) * CMP_STRIDE
        overlap = ((ccol < srow + SEL_BLOCK) & (ccol + CMP_LEN > srow)).astype(F32)
        score = jnp.dot(overlap, imp, precision=lax.Precision.HIGHEST, preferred_element_type=F32)
        jidx = lax.broadcasted_iota(I32, (LANES, TQ), 0)
        qb = qpos_t // SEL_BLOCK
        forced = (jidx == 0) | (jidx == qb) | (jidx == qb - 1)
        causal = jidx <= qb
        sc = jnp.where(forced, jnp.inf, jnp.where(causal, score, -jnp.inf))
        sel = (_topk_mask_t(sc, n_sel, SEL_TOPK) & causal).astype(BF16)

        def picked(key0, n, dist, sel=sel):
            e = _expand_blocks_t(sel, key0, n, SEL_BLOCK)
            return jnp.concatenate([e] * hg, axis=1) > 0.5

        keys = lambda key0, n: pl.ds(pl.multiple_of(key0, TQ), n)
        o_s = _flash_t(qt, lambda key0, n, g=g: k_ref[0, g, keys(key0, n), :],
                       lambda key0, n, g=g: kvt_ref[0, 3 * gw + g * HEAD_DIM:3 * gw + (g + 1) * HEAD_DIM, keys(key0, n)],
                       [(0, qi, picked, FLASH_GROUP),
                        (qi, qi + 1, lambda key0, n, dist: picked(key0, n, dist) & (dist() >= 0), 1)],
                       TQ, qpos, slope, lambda n, g=g: rise_ref[g, :n, :], m0, l0, m_ref, l_ref, acc_ref)

        back = -(-(WIN_A - 1) // TQ)
        o_w = _flash_t(qt, lambda key0, n, g=g: k_ref[0, KV_A + g, keys(key0, n), :],
                       lambda key0, n, g=g: kvt_ref[0, 5 * gw + g * HEAD_DIM:5 * gw + (g + 1) * HEAD_DIM, keys(key0, n)],
                       [(jnp.maximum(qi - back, 0), jnp.maximum(qi - back + 1, 0),
                         lambda key0, n, dist: dist() < WIN_A, 1),
                        (jnp.maximum(qi - back + 1, 0), qi, None, back - 1),
                        (qi, qi + 1, lambda key0, n, dist: dist() >= 0, 1)],
                       TQ, qpos, slope, lambda n, g=g: rise_ref[g, :n, :], m0, l0, m_ref, l_ref, acc_ref)

        gates = gt_ref[0]
        grow = lambda br: jnp.concatenate([gates[br * N_HEADS + h:br * N_HEADS + h + 1, :] for h in heads], axis=1)
        _store_group(ot_ref, g, hg, grow(0) * o_c + grow(1) * o_s + grow(2) * o_w)


def nsa_prompt_attn(slopes, q_t, k_rows, kv_t, kc, vc_t, gates_t):
    n, _, t = q_t.shape
    hg = N_HEADS // KV_A
    ncp = kc.shape[2]
    cols = hg * TQ
    return pl.pallas_call(
        functools.partial(_nsa_prompt_kernel, t_len=t),
        grid_spec=pltpu.PrefetchScalarGridSpec(
            num_scalar_prefetch=1, grid=(n, t // TQ),
            in_specs=[pl.BlockSpec((1, Q_WIDTH, TQ), lambda i, j, s: (i, 0, j)),
                      pl.BlockSpec((1, 2 * KV_A, t, HEAD_DIM), lambda i, j, s: (i, 0, 0, 0)),
                      pl.BlockSpec((1, kv_t.shape[1], t), lambda i, j, s: (i, 0, 0)),
                      pl.BlockSpec((1, KV_A, ncp, HEAD_DIM), lambda i, j, s: (i, 0, 0, 0)),
                      pl.BlockSpec((1, KV_A, HEAD_DIM, ncp), lambda i, j, s: (i, 0, 0, 0)),
                      pl.BlockSpec((1, LANES, TQ), lambda i, j, s: (i, 0, j)),
                      pl.BlockSpec((KV_A, FLASH_GROUP * TQ, cols), lambda i, j, s: (0, 0, 0))],
            out_specs=pl.BlockSpec((1, Q_WIDTH, TQ), lambda i, j, s: (i, 0, j)),
            scratch_shapes=[pltpu.VMEM((1, cols), F32), pltpu.VMEM((1, cols), F32),
                            pltpu.VMEM((HEAD_DIM, cols), F32)]),
        out_shape=jax.ShapeDtypeStruct((n, Q_WIDTH, t), BF16),
        compiler_params=_cparams(("parallel", "arbitrary")), name="nsa_prompt_attn",
    )(slopes, q_t, k_rows, kv_t, kc, vc_t, gates_t, _alibi_rise(slopes, KV_A, FLASH_GROUP * TQ))


def _swa_prompt_kernel(slopes_ref, sinks_ref, qt_ref, k_ref, kvt_ref, rise_ref, ot_ref, m_ref, l_ref, acc_ref):
    qi = pl.program_id(1)
    hg = N_HEADS // KV_B
    cols = hg * TQ
    gw = KV_B * HEAD_DIM
    qpos = qi * TQ + (lax.broadcasted_iota(I32, (1, cols), 1) & (TQ - 1))
    back = -(-(WIN_B - 1) // TQ)
    segments = [(jnp.maximum(qi - back, 0), qi, lambda key0, n, dist: dist() < WIN_B, 1),
                (qi, qi + 1, lambda key0, n, dist: dist() >= 0, 1)]
    keys = lambda key0, n: pl.ds(pl.multiple_of(key0, TQ), n)
    for g in range(KV_B):
        heads = range(g * hg, (g + 1) * hg)
        o = _flash_t(_group_queries(qt_ref, g, hg), lambda key0, n, g=g: k_ref[0, g, keys(key0, n), :],
                     lambda key0, n, g=g: kvt_ref[0, gw + g * HEAD_DIM:gw + (g + 1) * HEAD_DIM, keys(key0, n)],
                     segments, TQ, qpos, _row(slopes_ref, heads, TQ), lambda n, g=g: rise_ref[g, :n, :],
                     _row(sinks_ref, heads, TQ), jnp.ones((1, cols), F32), m_ref, l_ref, acc_ref)
        _store_group(ot_ref, g, hg, o)


def swa_prompt_attn(slopes, sinks, q_t, k_rows, kv_t):
    n, _, t = q_t.shape
    hg = N_HEADS // KV_B
    cols = hg * TQ
    return pl.pallas_call(
        _swa_prompt_kernel,
        grid_spec=pltpu.PrefetchScalarGridSpec(
            num_scalar_prefetch=2, grid=(n, t // TQ),
            in_specs=[pl.BlockSpec((1, Q_WIDTH, TQ), lambda i, j, s, k: (i, 0, j)),
                      pl.BlockSpec((1, KV_B, t, HEAD_DIM), lambda i, j, s, k: (i, 0, 0, 0)),
                      pl.BlockSpec((1, kv_t.shape[1], t), lambda i, j, s, k: (i, 0, 0)),
                      pl.BlockSpec((KV_B, TQ, cols), lambda i, j, s, k: (0, 0, 0))],
            out_specs=pl.BlockSpec((1, Q_WIDTH, TQ), lambda i, j, s, k: (i, 0, j)),
            scratch_shapes=[pltpu.VMEM((1, cols), F32), pltpu.VMEM((1, cols), F32),
                            pltpu.VMEM((HEAD_DIM, cols), F32)]),
        out_shape=jax.ShapeDtypeStruct((n, Q_WIDTH, t), BF16),
        compiler_params=_cparams(("parallel", "arbitrary")), name="swa_prompt_attn",
    )(slopes, sinks, q_t, k_rows, kv_t, _alibi_rise(slopes, KV_B, TQ))


def _block_mean_prompt_kernel(kt_ref, o_ref, *, nb):
    lane = lax.broadcasted_iota(I32, o_ref.shape[1:], 1)
    out = jnp.zeros(o_ref.shape[1:], F32)
    for b in range(nb):
        blk = kt_ref[0, :, b * MOBA_BLOCK:(b + 1) * MOBA_BLOCK]
        out = jnp.where(lane == b, jnp.sum(blk, axis=1, keepdims=True) / MOBA_BLOCK, out)
    o_ref[0] = out


def block_means_prompt(kv_t):
    n, _, t = kv_t.shape
    w = KV_C * HEAD_DIM
    return pl.pallas_call(
        functools.partial(_block_mean_prompt_kernel, nb=t // MOBA_BLOCK),
        grid=(n,),
        in_specs=[pl.BlockSpec((1, w, t), lambda i: (i, 0, 0))],
        out_specs=pl.BlockSpec((1, w, LANES), lambda i: (i, 0, 0)),
        out_shape=jax.ShapeDtypeStruct((n, w, LANES), F32),
        compiler_params=_cparams(("parallel",)), name="block_means_prompt",
    )(kv_t)


def _moba_prompt_kernel(slopes_ref, qt_ref, k_ref, kvt_ref, kmh_ref, kml_ref, rise_ref, ot_ref,
                        m_ref, l_ref, acc_ref, sel_ref,
                        *, t_len):
    qi = pl.program_id(1)
    q0 = qi * TQ
    hg = N_HEADS // KV_C
    cols = hg * TQ
    tk = MOBA_BLOCK
    gw = KV_C * HEAD_DIM
    nb = t_len // MOBA_BLOCK
    qpos = q0 + (lax.broadcasted_iota(I32, (1, cols), 1) & (TQ - 1))
    m0 = jnp.full((1, cols), M_INIT, F32)
    l0 = jnp.zeros((1, cols), F32)
    bidx = lax.broadcasted_iota(I32, (LANES, cols), 0)
    qb = qpos // MOBA_BLOCK
    keys = lambda key0, n: pl.ds(pl.multiple_of(key0, tk), n)
    for g in range(KV_C):
        heads = range(g * hg, (g + 1) * hg)
        qt = _group_queries(qt_ref, g, hg)
        gate = _mm(kmh_ref[0, g], qt) + _mm(kml_ref[0, g], qt)
        past_blk = bidx < qb
        sc = jnp.where(past_blk, gate, -jnp.inf)
        sel_ref[...] = ((_topk_mask_t(sc, nb, MOBA_TOPK) & past_blk) | (bidx == qb)).astype(F32)

        def picked(key0, n, dist):
            first = sel_ref[pl.ds(key0 // tk, 1), :]
            if n == tk:
                return first > 0.5
            second = sel_ref[pl.ds(key0 // tk + 1, 1), :]
            return jnp.where(lax.broadcasted_iota(I32, (n, cols), 0) < tk, first, second) > 0.5

        last = (q0 + TQ - 1) // tk
        o = _flash_t(qt, lambda key0, n, g=g: k_ref[0, g, keys(key0, n), :],
                     lambda key0, n, g=g: kvt_ref[0, gw + g * HEAD_DIM:gw + (g + 1) * HEAD_DIM, keys(key0, n)],
                     [(0, last, picked, MOBA_GROUP), (last, last + 1, lambda key0, n, dist: dist() >= 0, 1)],
                     tk, qpos, _row(slopes_ref, heads, TQ), lambda n, g=g: rise_ref[g, :n, :],
                     m0, l0, m_ref, l_ref, acc_ref)
        _store_group(ot_ref, g, hg, o)


def moba_prompt_attn(slopes, q_t, k_rows, kv_t, kmh, kml):
    n, _, t = q_t.shape
    hg = N_HEADS // KV_C
    cols = hg * TQ
    return pl.pallas_call(
        functools.partial(_moba_prompt_kernel, t_len=t),
        grid_spec=pltpu.PrefetchScalarGridSpec(
            num_scalar_prefetch=1, grid=(n, t // TQ),
            in_specs=[pl.BlockSpec((1, Q_WIDTH, TQ), lambda i, j, s: (i, 0, j)),
                      pl.BlockSpec((1, KV_C, t, HEAD_DIM), lambda i, j, s: (i, 0, 0, 0)),
                      pl.BlockSpec((1, kv_t.shape[1], t), lambda i, j, s: (i, 0, 0)),
                      pl.BlockSpec((1, KV_C, LANES, HEAD_DIM), lambda i, j, s: (i, 0, 0, 0)),
                      pl.BlockSpec((1, KV_C, LANES, HEAD_DIM), lambda i, j, s: (i, 0, 0, 0)),
                      pl.BlockSpec((KV_C, MOBA_GROUP * MOBA_BLOCK, cols), lambda i, j, s: (0, 0, 0))],
            out_specs=pl.BlockSpec((1, Q_WIDTH, TQ), lambda i, j, s: (i, 0, j)),
            scratch_shapes=[pltpu.VMEM((1, cols), F32), pltpu.VMEM((1, cols), F32),
                            pltpu.VMEM((HEAD_DIM, cols), F32), pltpu.VMEM((LANES, cols), F32)]),
        out_shape=jax.ShapeDtypeStruct((n, Q_WIDTH, t), BF16),
        compiler_params=_cparams(("parallel", "arbitrary")), name="moba_prompt_attn",
    )(slopes, q_t, k_rows, kv_t, kmh, kml, _alibi_rise(slopes, KV_C, MOBA_GROUP * MOBA_BLOCK))


def _sample_cols(slopes_ref, s_len, past):
    slope = _col(slopes_ref, range(N_HEADS), s_len)
    qpos = past + (lax.broadcasted_iota(I32, (N_HEADS * s_len, 1), 0) % s_len)
    return slope, qpos


def _window_sample(qx, kt_all, vt_all, qpos, slope, wb, window, past, sink=None):
    kpos = past - wb + lax.broadcasted_iota(I32, (1, kt_all.shape[1]), 1)
    dist = qpos - kpos
    s = _mm(qx, kt_all.astype(BF16))
    s = jnp.where((dist >= 0) & (dist < window) & (kpos >= 0), s - slope * dist.astype(F32), -jnp.inf)
    p = _softmax_rows(s, sink)
    return _mm_nt(p.astype(BF16), vt_all.astype(BF16))


def _nsa_sample_small_kernel(slopes_ref, qx_ref, kc_ref, vc_ref, kw_ref, vw_ref, oc_ref, ow_ref, sel_ref,
                             *, s_len, past, wb):
    hg = N_HEADS // KV_A
    slope, qpos = _sample_cols(slopes_ref, s_len, past)
    qx = qx_ref[0]
    ncp = kc_ref.shape[1]
    n_cmp = past // CMP_STRIDE - CMP_LEN // CMP_STRIDE + 1
    n_sel = -(-(past + s_len) // SEL_BLOCK)
    nbl = sel_ref.shape[2]

    cidx = lax.broadcasted_iota(I32, (1, ncp), 1)
    dist = qpos - (cidx * CMP_STRIDE + CMP_LEN - 1)
    s = _mm_nt(qx, kc_ref[0])
    s = jnp.where((dist >= 0) & (cidx < n_cmp), s - slope * dist.astype(F32), -jnp.inf)
    p = _softmax_rows(s)
    oc_ref[0] = _mm(p.astype(BF16), vc_ref[0])

    imps = []
    for g in range(KV_A):
        base = g * hg * s_len
        imp = p[base:base + s_len]
        for h in range(1, hg):
            imp = imp + p[base + h * s_len:base + (h + 1) * s_len]
        imps.append(imp)
    imp = jnp.concatenate(imps, axis=0)
    crow = lax.broadcasted_iota(I32, (ncp, nbl), 0) * CMP_STRIDE
    scol = lax.broadcasted_iota(I32, (ncp, nbl), 1) * SEL_BLOCK
    overlap = ((crow < scol + SEL_BLOCK) & (crow + CMP_LEN > scol)
               & (crow < n_cmp * CMP_STRIDE)).astype(F32)
    score = jnp.dot(imp, overlap, precision=lax.Precision.HIGHEST, preferred_element_type=F32)
    jidx = lax.broadcasted_iota(I32, (1, nbl), 1)
    tpos = past + (lax.broadcasted_iota(I32, (KV_A * s_len, 1), 0) % s_len)
    qb = tpos // SEL_BLOCK
    forced = (jidx == 0) | (jidx == qb) | (jidx == qb - 1)
    causal = jidx <= qb
    sc = jnp.where(forced, jnp.inf, jnp.where(causal, score, -jnp.inf))
    sel = (_topk_mask(sc, n_sel, SEL_TOPK) & causal).astype(F32)
    sel_ref[0] = jnp.concatenate([sel[g * s_len:(g + 1) * s_len] for g in range(KV_A) for _ in range(hg)], axis=0)

    ow_ref[0] = _window_sample(qx, kw_ref[0], vw_ref[0], qpos, slope, wb, WIN_A, past)


def nsa_sample_small(slopes, qx, kcmp, vcmp, kwt_all, vwt_all, past, wb, nbl):
    n, rows, w = qx.shape
    s_len = rows // N_HEADS
    ncp = kcmp.shape[1]
    lw = kwt_all.shape[2]
    per_seq = lambda shape: pl.BlockSpec((1,) + shape, lambda i, s: (i, 0, 0))
    return pl.pallas_call(
        functools.partial(_nsa_sample_small_kernel, s_len=s_len, past=past, wb=wb),
        grid_spec=pltpu.PrefetchScalarGridSpec(
            num_scalar_prefetch=1, grid=(n,),
            in_specs=[per_seq((rows, w)), per_seq((ncp, w)), per_seq((ncp, w)), per_seq((w, lw)), per_seq((w, lw))],
            out_specs=[per_seq((rows, w)), per_seq((rows, w)), per_seq((rows, nbl))]),
        out_shape=[jax.ShapeDtypeStruct((n, rows, w), F32), jax.ShapeDtypeStruct((n, rows, w), F32),
                   jax.ShapeDtypeStruct((n, rows, nbl), F32)],
        compiler_params=_cparams(("parallel",)), name="nsa_sample_small",
    )(slopes, qx, kcmp, vcmp, kwt_all, vwt_all)


def _swa_sample_kernel(slopes_ref, sinks_ref, qx_ref, k_ref, v_ref, o_ref, *, s_len, past, wb):
    slope, qpos = _sample_cols(slopes_ref, s_len, past)
    sink = _col(sinks_ref, range(N_HEADS), s_len)
    o_ref[0] = _window_sample(qx_ref[0], k_ref[0], v_ref[0], qpos, slope, wb, WIN_B, past, sink)


def swa_sample_attn(slopes, sinks, qx, kt_all, vt_all, past, wb):
    n, rows, w = qx.shape
    s_len = rows // N_HEADS
    lw = kt_all.shape[2]
    per_seq = lambda shape: pl.BlockSpec((1,) + shape, lambda i, s, k: (i, 0, 0))
    return pl.pallas_call(
        functools.partial(_swa_sample_kernel, s_len=s_len, past=past, wb=wb),
        grid_spec=pltpu.PrefetchScalarGridSpec(
            num_scalar_prefetch=2, grid=(n,),
            in_specs=[per_seq((rows, w)), per_seq((w, lw)), per_seq((w, lw))],
            out_specs=per_seq((rows, w))),
        out_shape=jax.ShapeDtypeStruct((n, rows, w), F32),
        compiler_params=_cparams(("parallel",)), name="swa_sample_attn",
    )(slopes, sinks, qx, kt_all, vt_all)


def _paged_attn_kernel(ptab_ref, slopes_ref, qx_ref, *refs, layer, n_pages, bs, tks, s_len, past, from_means):
    if from_means:
        knew_ref, vnew_ref, kpool, vpool, o_ref, kbuf, vbuf, sem, m_ref, l_ref, acc_ref, selb_ref = refs
    else:
        sel_ref, knew_ref, vnew_ref, kpool, vpool, o_ref, kbuf, vbuf, sem, m_ref, l_ref, acc_ref, selb_ref = refs
    slot = _paged_pipeline(ptab_ref, (kpool, vpool), (kbuf, vbuf), sem, layer, 1, n_pages, True)
    slope, qpos = _sample_cols(slopes_ref, s_len, past)
    qx = qx_ref[0]
    nbw = selb_ref.shape[1] - LANES

    m_ref[...] = jnp.full_like(m_ref, M_INIT)
    l_ref[...] = jnp.zeros_like(l_ref)
    acc_ref[...] = jnp.zeros_like(acc_ref)
    selb_ref[:, nbw:] = jnp.zeros((selb_ref.shape[0], LANES), F32)
    if from_means:
        lane = lax.broadcasted_iota(I32, (kbuf.shape[1], nbw), 1)
        km = jnp.zeros((kbuf.shape[1], nbw), F32)
        for b in range(past // bs):
            blk = kbuf[slot, :, b * bs:(b + 1) * bs]
            km = jnp.where(lane == b, jnp.sum(blk, axis=1, keepdims=True) / bs, km)
        kmh = km.astype(BF16)
        kml = (km - kmh.astype(F32)).astype(BF16)
        gate = _mm(qx, kmh) + _mm(qx, kml)
        bidx = lax.broadcasted_iota(I32, (1, km.shape[1]), 1)
        past_blk = bidx < qpos // bs
        sc = jnp.where(past_blk, gate, -jnp.inf)
        selb_ref[:, :nbw] = (_topk_mask(sc, past // bs, MOBA_TOPK) & past_blk).astype(F32)
    else:
        selb_ref[:, :nbw] = sel_ref[0]

    def update(s, vt):
        m_old = m_ref[...]
        m_new = jnp.maximum(m_old, jnp.max(s, axis=1, keepdims=True))
        alpha = jnp.exp(m_old - m_new)
        p = jnp.exp(s - m_new)
        l_ref[...] = alpha * l_ref[...] + jnp.sum(p, axis=1, keepdims=True)
        acc_ref[...] = alpha * acc_ref[...] + _mm_nt(p.astype(BF16), vt)
        m_ref[...] = m_new

    key_in_tile = lax.broadcasted_iota(I32, (1, tks), 1)
    onehot = (lax.broadcasted_iota(I32, (LANES, tks), 1) // bs
              == lax.broadcasted_iota(I32, (LANES, tks), 0)).astype(BF16)
    rise = slope * key_in_tile.astype(F32)
    for t in range(n_pages * PAGE // tks):
        kt = kbuf[slot, :, t * tks:(t + 1) * tks].astype(BF16)
        vt = vbuf[slot, :, t * tks:(t + 1) * tks].astype(BF16)
        blk0 = t * (tks // bs)
        keep = _mm(selb_ref[:, blk0:blk0 + LANES].astype(BF16), onehot) > 0.5
        s = _mm(qx, kt) + rise - slope * (qpos - t * tks).astype(F32)
        update(jnp.where(keep, s, -jnp.inf), vt)

    j = lax.broadcasted_iota(I32, (1, knew_ref.shape[2]), 1)
    dist = qpos - (past + j)
    s = _mm(qx, knew_ref[0].astype(BF16))
    s = jnp.where((dist >= 0) & (j < s_len), s - slope * dist.astype(F32), -jnp.inf)
    update(s, vnew_ref[0].astype(BF16))
    o_ref[0] = acc_ref[...] / jnp.maximum(l_ref[...], TINY)


def paged_attn(ptab, slopes, qx, sel, knew_t, vnew_t, pool_k, pool_v, layer, *, bs, tks, past):
    n, n_pages = ptab.shape
    _, rows, w = qx.shape
    s_len = rows // N_HEADS
    from_means = sel is None
    nbw = LANES if from_means else sel.shape[2]
    per_seq = lambda shape: pl.BlockSpec((1,) + shape, lambda i, c, pt, sl: (i, 0, 0))
    anyspec = pl.BlockSpec(memory_space=pl.ANY)
    sel_spec, sel_arg = ([], []) if from_means else ([per_seq(sel.shape[1:])], [sel])
    return pl.pallas_call(
        functools.partial(_paged_attn_kernel, layer=layer, n_pages=n_pages, bs=bs, tks=tks, s_len=s_len, past=past,
                          from_means=from_means),
        grid_spec=pltpu.PrefetchScalarGridSpec(
            num_scalar_prefetch=2, grid=(n, 1),
            in_specs=[per_seq((rows, w))] + sel_spec + [per_seq(knew_t.shape[1:]),
                      per_seq(vnew_t.shape[1:]), anyspec, anyspec],
            out_specs=per_seq((rows, w)),
            scratch_shapes=[pltpu.VMEM((2, w, n_pages * PAGE), F32), pltpu.VMEM((2, w, n_pages * PAGE), F32),
                            pltpu.SemaphoreType.DMA((2, 2)),
                            pltpu.VMEM((rows, 1), F32), pltpu.VMEM((rows, 1), F32), pltpu.VMEM((rows, w), F32),
                            pltpu.VMEM((rows, nbw + LANES), F32)]),
        out_shape=jax.ShapeDtypeStruct((n, rows, w), F32),
        compiler_params=_cparams(("arbitrary", "arbitrary")), name="paged_attn",
    )(ptab, slopes, qx, *sel_arg, knew_t, vnew_t, pool_k, pool_v)


def _stack_queries(q, n, s_len, groups):
    hg = N_HEADS // groups
    qh = q.reshape(n, s_len, groups, hg, 1, HEAD_DIM).transpose(0, 2, 3, 1, 4, 5)
    eye = jnp.eye(groups, dtype=q.dtype).reshape(1, groups, 1, 1, groups, 1)
    return (qh * eye).reshape(n, N_HEADS * s_len, groups * HEAD_DIM)


def _unstack_outputs(o, n, s_len, groups):
    hg = N_HEADS // groups
    o6 = o.reshape(n, groups, hg, s_len, groups, HEAD_DIM)
    diag = jnp.stack([o6[:, g, :, :, g, :] for g in range(groups)], axis=1)
    return diag.transpose(0, 3, 1, 2, 4).reshape(n * s_len, Q_WIDTH)


def _feature_major(cache):
    l, p, r, g, d = cache.shape
    return cache.transpose(0, 1, 3, 4, 2).reshape(l, p, g * d, r)


def _state(kv_t, branch, groups):
    n, _, t = kv_t.shape
    gw = groups * HEAD_DIM
    return kv_t[:, branch * gw:(branch + 1) * gw].reshape(n, groups, HEAD_DIM, t).transpose(0, 3, 1, 2)


def _window_with_new(buf_t, new_rows, groups):
    n, w, wb = buf_t.shape
    s_len = new_rows.shape[1]
    both = jnp.concatenate([buf_t, new_rows.transpose(0, 2, 1)], axis=2)
    state = both[:, :, s_len:].reshape(n, groups, HEAD_DIM, wb).transpose(0, 3, 1, 2)
    lw = -(-(wb + s_len) // LANES) * LANES
    return jnp.pad(both, ((0, 0), (0, 0), (0, lw - wb - s_len))), state


def _new_rows_t(rows):
    return jnp.pad(rows.transpose(0, 2, 1), ((0, 0), (0, 0), (0, LANES - rows.shape[1])))


def _nsa_layer(xp, xs, g_pre, w_in, cw, layer, caches, ptab, slopes):
    cmp_k, cmp_v, sel_k, sel_v, win_k, win_v = caches
    n_p, t, _ = xp.shape
    n_s = ptab.shape[0]
    s_len = xs.shape[0] // n_s
    past = ptab.shape[1] * PAGE
    gw = KV_A * HEAD_DIM

    q_t, kv_t, kv_tb, k_rows, cmp_rows, gates_t = proj_prompt(xp, g_pre, w_in, (4, 5, 8, 9), True)
    kcmp, vcmp = compress_prompt(cmp_rows, cw)
    ncp = max(kcmp.shape[1], LANES)
    pad = ((0, 0), (0, ncp - kcmp.shape[1]), (0, 0))
    kc = jnp.pad(kcmp, pad).reshape(n_p, ncp, KV_A, HEAD_DIM).transpose(0, 2, 1, 3)
    vc_t = jnp.pad(vcmp, pad).reshape(n_p, ncp, KV_A, HEAD_DIM).transpose(0, 2, 3, 1)
    o_t = nsa_prompt_attn(slopes, q_t, k_rows, kv_tb, kc, vc_t, gates_t)
    wb = min(WIN_A, t)
    st_p = tuple(_state(kv_t, i, KV_A) for i in range(4)) + tuple(_state(kv_t[:, :, t - wb:], i, KV_A) for i in (4, 5))

    q, kv, gates = proj_sample(xs, g_pre, w_in, True)
    kv4 = kv.reshape(n_s, s_len, 6, gw)
    kwt_all, st_wk = _window_with_new(_feature_major(win_k)[layer], kv4[:, :, 4], KV_A)
    vwt_all, st_wv = _window_with_new(_feature_major(win_v)[layer], kv4[:, :, 5], KV_A)
    kcmp, vcmp = compress_sample(ptab, _feature_major(cmp_k), _feature_major(cmp_v), layer, cw)
    qx = _stack_queries(q, n_s, s_len, KV_A)
    nbl = -(-(-(-(past + s_len) // SEL_BLOCK)) // LANES) * LANES
    o_c, o_w, sel = nsa_sample_small(slopes, qx, kcmp, vcmp, kwt_all, vwt_all, past, win_k.shape[2], nbl)
    o_s = paged_attn(ptab, slopes, qx, sel, _new_rows_t(kv4[:, :, 2]), _new_rows_t(kv4[:, :, 3]),
                     _feature_major(sel_k), _feature_major(sel_v), layer,
                     bs=SEL_BLOCK, tks=min(PAGED_TK, past), past=past)
    o_s_list = [_unstack_outputs(o, n_s, s_len, KV_A) for o in (o_c, o_s, o_w)]
    gate_list = [jnp.repeat(gates[:, br * N_HEADS:(br + 1) * N_HEADS], HEAD_DIM, axis=1) for br in range(3)]
    kv6 = kv.reshape(n_s, s_len, 6, KV_A, HEAD_DIM)
    st_s = tuple(kv6[:, :, i] for i in range(4)) + (st_wk, st_wv)
    return o_t, o_s_list, gate_list, st_p, st_s


def _swa_layer(xp, xs, g_pre, w_in, sinks, layer, caches, past, slopes, n_s):
    buf_k, buf_v = caches
    n_p, t, _ = xp.shape
    s_len = xs.shape[0] // n_s
    gw = KV_B * HEAD_DIM

    q_t, kv_t, kv_tb, k_rows = proj_prompt(xp, g_pre, w_in, (0, 1), False)
    o_t = swa_prompt_attn(slopes, sinks, q_t, k_rows, kv_tb)
    wb = min(WIN_B, t)
    st_p = tuple(_state(kv_t[:, :, t - wb:], i, KV_B) for i in (0, 1))

    q, kv = proj_sample(xs, g_pre, w_in, False)
    kv4 = kv.reshape(n_s, s_len, 2, gw)
    kt_all, st_k = _window_with_new(_feature_major(buf_k)[layer], kv4[:, :, 0], KV_B)
    vt_all, st_v = _window_with_new(_feature_major(buf_v)[layer], kv4[:, :, 1], KV_B)
    o = swa_sample_attn(slopes, sinks, _stack_queries(q, n_s, s_len, KV_B), kt_all, vt_all, past, buf_k.shape[2])
    return o_t, [_unstack_outputs(o, n_s, s_len, KV_B)], st_p, (st_k, st_v)


def _split_hi_lo(x):
    hi = x.astype(BF16)
    return hi, (x - hi.astype(F32)).astype(BF16)


def _moba_layer(xp, xs, g_pre, w_in, layer, caches, ptab, slopes):
    pool_k, pool_v = caches
    n_p, t, _ = xp.shape
    n_s = ptab.shape[0]
    s_len = xs.shape[0] // n_s
    gw = KV_C * HEAD_DIM
    past = ptab.shape[1] * PAGE

    q_t, kv_t, kv_tb, k_rows = proj_prompt(xp, g_pre, w_in, tuple(range(KV_C)), False)
    kmean = block_means_prompt(kv_t).reshape(n_p, KV_C, HEAD_DIM, LANES).transpose(0, 1, 3, 2)
    kmh, kml = _split_hi_lo(kmean)
    o_t = moba_prompt_attn(slopes, q_t, k_rows, kv_tb, kmh, kml)
    st_p = (_state(kv_t, 0, KV_C), _state(kv_t, 1, KV_C))

    q, kv = proj_sample(xs, g_pre, w_in, False)
    kv4 = kv.reshape(n_s, s_len, 2, gw)
    pk_t, pv_t = _feature_major(pool_k), _feature_major(pool_v)
    o = paged_attn(ptab, slopes, _stack_queries(q, n_s, s_len, KV_C), None,
                   _new_rows_t(kv4[:, :, 0]), _new_rows_t(kv4[:, :, 1]), pk_t, pv_t, layer,
                   bs=MOBA_BLOCK, tks=min(PAGED_TK, past), past=past)
    kv5 = kv.reshape(n_s, s_len, 2, KV_C, HEAD_DIM)
    return o_t, [_unstack_outputs(o, n_s, s_len, KV_C)], st_p, (kv5[:, :, 0], kv5[:, :, 1])


def kernel(x_prompt, x_sample, cache_nsa_cmp_k, cache_nsa_cmp_v, cache_nsa_sel_k, cache_nsa_sel_v, cache_nsa_win_k, cache_nsa_win_v, cache_swa_k, cache_swa_v, cache_moba_k, cache_moba_v, page_table, norm_pre, norm_post, ffn_w_gate, ffn_w_up, ffn_w_down, nsa_w_in, nsa_cmp_pos_k, nsa_cmp_w1_k, nsa_cmp_w2_k, nsa_cmp_pos_v, nsa_cmp_w1_v, nsa_cmp_w2_v, nsa_w_out, swa_w_in, swa_sinks, swa_w_out, moba_w_in, moba_w_out):
    n_p, t, d = x_prompt.shape
    n_s, s_len, _ = x_sample.shape
    depth = norm_pre.shape[0]
    slopes = jnp.exp2(-8.0 * jnp.arange(1, N_HEADS + 1, dtype=F32) / N_HEADS)
    past = page_table.shape[1] * PAGE
    xp = x_prompt.reshape(n_p * t, d)
    xs = x_sample.reshape(n_s * s_len, d)
    nsa_p, nsa_s, swa_p, swa_s, moba_p, moba_s = [], [], [], [], [], []

    def ffn_both(xp, xs, i, which, slot):
        wg, wu, wd = (w[i, which].astype(BF16) for w in (ffn_w_gate, ffn_w_up, ffn_w_down))
        return (ffn(xp, norm_pre[i, slot], norm_post[i, slot], wg, wu, wd),
                ffn(xs, norm_pre[i, slot], norm_post[i, slot], wg, wu, wd))

    for i in range(depth):
        kind, j = i % 3, i // 3
        xp, xs = ffn_both(xp, xs, i, 0, 0)
        xp3 = xp.reshape(n_p, t, d)
        gates_s = None
        if kind == 0:
            cw = {"k": _compress_weights(nsa_cmp_pos_k[j], nsa_cmp_w1_k[j], nsa_cmp_w2_k[j]),
                  "v": _compress_weights(nsa_cmp_pos_v[j], nsa_cmp_w1_v[j], nsa_cmp_w2_v[j])}
            o_t, o_s, gates_s, st_p, st_s = _nsa_layer(
                xp3, xs, norm_pre[i, 1], nsa_w_in[j], cw, j,
                (cache_nsa_cmp_k, cache_nsa_cmp_v, cache_nsa_sel_k, cache_nsa_sel_v, cache_nsa_win_k, cache_nsa_win_v),
                page_table, slopes)
            nsa_p.append(st_p)
            nsa_s.append(st_s)
            w_out = nsa_w_out[j]
        elif kind == 1:
            o_t, o_s, st_p, st_s = _swa_layer(xp3, xs, norm_pre[i, 1], swa_w_in[j], swa_sinks[j], j,
                                              (cache_swa_k, cache_swa_v), past, slopes, n_s)
            swa_p.append(st_p)
            swa_s.append(st_s)
            w_out = swa_w_out[j]
        else:
            o_t, o_s, st_p, st_s = _moba_layer(xp3, xs, norm_pre[i, 1], moba_w_in[j], j,
                                               (cache_moba_k, cache_moba_v), page_table, slopes)
            moba_p.append(st_p)
            moba_s.append(st_s)
            w_out = moba_w_out[j]
        w_out = w_out.astype(BF16)
        xp = out_proj_prompt(o_t, w_out, norm_post[i, 1], xp3).reshape(n_p * t, d)
        xs = out_proj(o_s, w_out, norm_post[i, 1], xs, gates_s)
        xp, xs = ffn_both(xp, xs, i, 1, 2)

    stack = lambda states: [jnp.stack(a) for a in zip(*states)]
    return (xp.reshape(n_p, t, d), xs.reshape(n_s, s_len, d),
            *stack(nsa_p), *stack(swa_p), *stack(moba_p), *stack(nsa_s), *stack(swa_s), *stack(moba_s))
```
